```python
import jax, jax.numpy as jnp
from jax import lax
import numpy as np

D_MODEL = 2048
BATCH = 4
SEQ = 4096
DEPTH = 1

MIX_WIDTH = D_MODEL
HEAD_DIM = 128
DSWA_HEADS = (MIX_WIDTH // 2) // HEAD_DIM
DSWA_CONFIGS = ((128, 1), (512, 4), (2048, 16))
ROT_DIM = HEAD_DIM // 4
ROPE_THETA = 500000.0
MLA_HEADS = (MIX_WIDTH // 2) // HEAD_DIM
Q_LORA_RANK = 512
KV_LORA_RANK = 512
QK_NOPE_DIM = 128
QK_ROPE_DIM = 64
V_HEAD_DIM = 128
D_FF = 4 * D_MODEL
Q_BLOCK = 128
NORM_EPS = 1e-6
NEG_INF = -1e30

A_WIDTH = DSWA_HEADS * HEAD_DIM
IN_SPLITS = (A_WIDTH, 2 * A_WIDTH, 3 * A_WIDTH,
             3 * A_WIDTH + Q_LORA_RANK,
             3 * A_WIDTH + Q_LORA_RANK + KV_LORA_RANK)
IN_COLS = 3 * A_WIDTH + Q_LORA_RANK + KV_LORA_RANK + QK_ROPE_DIM
OUT_ROWS = DSWA_HEADS * HEAD_DIM + MLA_HEADS * V_HEAD_DIM

kernel_name = "hybrid_dilated_swa_mla_sandwich"


def _rmsnorm(x, gain):
    xf = x.astype(jnp.float32)
    xf = xf * lax.rsqrt(jnp.mean(xf * xf, axis=-1, keepdims=True) + NORM_EPS)
    return xf.astype(x.dtype) * gain


def _rope(x, positions, rot_dim):
    inv_freq = ROPE_THETA ** (-jnp.arange(0, rot_dim, 2, dtype=jnp.float32) / rot_dim)
    ang = positions.astype(jnp.float32)[..., None] * inv_freq
    cos = jnp.cos(ang)[:, :, None, :]
    sin = jnp.sin(ang)[:, :, None, :]
    xr = x[..., :rot_dim].astype(jnp.float32)
    x1, x2 = xr[..., : rot_dim // 2], xr[..., rot_dim // 2:]
    rot = jnp.concatenate([x1 * cos - x2 * sin, x2 * cos + x1 * sin], axis=-1)
    return jnp.concatenate([rot.astype(x.dtype), x[..., rot_dim:]], axis=-1)


def _dilated_window_attention(q, k, v, window, dilation):
    B, S, H, D = q.shape
    steps = window // dilation
    span = dilation * Q_BLOCK
    s_pad = -(-S // span) * span
    pad = ((0, 0), (0, s_pad - S), (0, 0), (0, 0))
    q, k, v = jnp.pad(q, pad), jnp.pad(k, pad), jnp.pad(v, pad)
    nb = s_pad // span
    qb = q.reshape(B, nb, Q_BLOCK, dilation, H, D)
    kb = k.reshape(B, nb, Q_BLOCK, dilation, H, D)
    vb = v.reshape(B, nb, Q_BLOCK, dilation, H, D)

    def with_prev(t):
        prev = jnp.pad(t[:, :-1], ((0, 0), (1, 0), (0, 0), (0, 0), (0, 0), (0, 0)))
        return jnp.concatenate([prev, t], axis=2)

    kc, vc = with_prev(kb), with_prev(vb)
    s = jnp.einsum('bniphd,bnjphd->bnphij', qb, kc).astype(jnp.float32) * (D ** -0.5)
    i = jnp.arange(Q_BLOCK)[:, None]
    j = jnp.arange(2 * Q_BLOCK)[None, :]
    dist = i + Q_BLOCK - j
    band = (dist >= 0) & (dist <= steps)
    first = (jnp.arange(nb) == 0)[:, None, None] & (j < Q_BLOCK)[None]
    mask = band[None] & ~first
    s = jnp.where(mask[None, :, None, None], s, NEG_INF)
    m = jnp.max(s, axis=-1, keepdims=True)
    p = jnp.exp(s - m)
    denom = jnp.sum(p, axis=-1, keepdims=True)
    o = jnp.einsum('bnphij,bnjphd->bniphd', (p / denom).astype(v.dtype), vc)
    lse = (m + jnp.log(denom))[..., 0]
    o = o.reshape(B, s_pad, H, D)[:, :S]
    lse = jnp.transpose(lse, (0, 1, 4, 2, 3)).reshape(B, s_pad, H)[:, :S]
    return o, lse


def _causal_block_attention(q, k, v, scale):
    B, S, H, Dk = q.shape
    nb = S // Q_BLOCK
    qb = q.reshape(B, nb, Q_BLOCK, H, Dk).transpose(1, 0, 2, 3, 4)
    key_pos = jnp.arange(S)

    def one_block(args):
        n, qn = args
        s = jnp.einsum('bqhd,bkhd->bhqk', qn, k).astype(jnp.float32) * scale
        q_pos = n * Q_BLOCK + jnp.arange(Q_BLOCK)
        s = jnp.where((key_pos[None, :] <= q_pos[:, None])[None, None], s, NEG_INF)
        p = jax.nn.softmax(s, axis=-1)
        return jnp.einsum('bhqk,bkhd->bqhd', p.astype(v.dtype), v)

    o = lax.map(one_block, (jnp.arange(nb), qb))
    return o.transpose(1, 0, 2, 3, 4).reshape(B, S, H, v.shape[-1])


def setup_inputs(seed: int = 0) -> dict:
    key = jax.random.key(seed)
    ks = jax.random.split(key, 16)
    f32 = jnp.float32

    def w(k, shape, fan_in):
        return jax.random.normal(k, shape, f32) * (fan_in ** -0.5)

    def gain(k, n):
        return 1.0 + 0.05 * jax.random.normal(k, (DEPTH, n), f32)

    x = jax.random.normal(ks[0], (BATCH, SEQ, D_MODEL), f32)
    offset = jax.random.randint(ks[1], (BATCH, 1), 0, 2048, dtype=jnp.int32)
    positions = offset + jnp.arange(SEQ, dtype=jnp.int32)[None, :]
    return {
        "x": x,
        "positions": positions,
        "norm_attn_pre": gain(ks[2], D_MODEL),
        "norm_attn_post": gain(ks[3], D_MODEL),
        "w_in": w(ks[4], (DEPTH, D_MODEL, IN_COLS), D_MODEL),
        "q_latent_norm": gain(ks[5], Q_LORA_RANK),
        "kv_latent_norm": gain(ks[6], KV_LORA_RANK),
        "w_uq": w(ks[7], (DEPTH, Q_LORA_RANK, MLA_HEADS * (QK_NOPE_DIM + QK_ROPE_DIM)), Q_LORA_RANK),
        "w_ukv": w(ks[8], (DEPTH, KV_LORA_RANK, MLA_HEADS * (QK_NOPE_DIM + V_HEAD_DIM)), KV_LORA_RANK),
        "w_out": w(ks[9], (DEPTH, OUT_ROWS, D_MODEL), OUT_ROWS),
        "norm_mlp_pre": gain(ks[10], D_MODEL),
        "norm_mlp_post": gain(ks[11], D_MODEL),
        "w_up": w(ks[12], (DEPTH, D_MODEL, D_FF), D_MODEL),
        "w_down": w(ks[13], (DEPTH, D_FF, D_MODEL), D_FF),
    }


def reference(x, positions, norm_attn_pre, norm_attn_post, w_in, q_latent_norm,
              kv_latent_norm, w_uq, w_ukv, w_out, norm_mlp_pre, norm_mlp_post,
              w_up, w_down):
    B, S, _ = x.shape
    for layer in range(DEPTH):
        h = _rmsnorm(x, norm_attn_pre[layer])
        proj = h @ w_in[layer]
        a_q, a_k, a_v, c_q, c_kv, k_r = jnp.split(proj, IN_SPLITS, axis=-1)

        a_q = _rope(a_q.reshape(B, S, DSWA_HEADS, HEAD_DIM), positions, ROT_DIM)
        a_k = _rope(a_k.reshape(B, S, DSWA_HEADS, HEAD_DIM), positions, ROT_DIM)
        a_v = a_v.reshape(B, S, DSWA_HEADS, HEAD_DIM)
        outs, lses = [], []
        for window, dilation in DSWA_CONFIGS:
            o, lse = _dilated_window_attention(a_q, a_k, a_v, window, dilation)
            outs.append(o)
            lses.append(lse)
        alpha = jax.nn.softmax(jnp.stack(lses, axis=0), axis=0)
        a_out = jnp.sum(alpha[..., None].astype(a_v.dtype) * jnp.stack(outs, axis=0), axis=0)

        c_q = _rmsnorm(c_q, q_latent_norm[layer])
        q_b = (c_q @ w_uq[layer]).reshape(B, S, MLA_HEADS, QK_NOPE_DIM + QK_ROPE_DIM)
        q_nope, q_rope = q_b[..., :QK_NOPE_DIM], q_b[..., QK_NOPE_DIM:]
        q_rope = _rope(q_rope, positions, QK_ROPE_DIM)
        c_kv = _rmsnorm(c_kv, kv_latent_norm[layer])
        kv = (c_kv @ w_ukv[layer]).reshape(B, S, MLA_HEADS, QK_NOPE_DIM + V_HEAD_DIM)
        k_nope, v_b = kv[..., :QK_NOPE_DIM], kv[..., QK_NOPE_DIM:]
        k_rope = _rope(k_r[:, :, None, :], positions, QK_ROPE_DIM)
        q_full = jnp.concatenate([q_nope, q_rope], axis=-1)
        k_full = jnp.concatenate(
            [k_nope, jnp.broadcast_to(k_rope, (B, S, MLA_HEADS, QK_ROPE_DIM))], axis=-1)
        b_out = _causal_block_attention(q_full, k_full, v_b,
                                        (QK_NOPE_DIM + QK_ROPE_DIM) ** -0.5)

        mixed = jnp.concatenate([a_out.reshape(B, S, DSWA_HEADS * HEAD_DIM),
                                 b_out.reshape(B, S, MLA_HEADS * V_HEAD_DIM)], axis=-1)
        x = x + _rmsnorm(mixed @ w_out[layer], norm_attn_post[layer])

        h = _rmsnorm(x, norm_mlp_pre[layer])
        u = jnp.square(jax.nn.relu(h @ w_up[layer]))
        x = x + _rmsnorm(u @ w_down[layer], norm_mlp_post[layer])
    return x
```

```python
import functools

import jax
import jax.numpy as jnp
from jax import lax
from jax.experimental import pallas as pl
from jax.experimental.pallas import tpu as pltpu

F32 = jnp.float32
BF16 = jnp.bfloat16

LANES = 128
HEAD_DIM = 128
ROT_DIM = HEAD_DIM // 4
ROPE_THETA = 500000.0
QK_NOPE_DIM = 128
QK_ROPE_DIM = 64
V_HEAD_DIM = 128
MLA_QK_PAD = 256
Q_BLOCK = 128
DILATIONS = (1, 4, 16)
NORM_EPS = 1e-6
NEG_INF = -1e30
VMEM_LIMIT = 56 * 1024 * 1024


def _rms(x, gain):
    ms = jnp.mean(x * x, axis=-1, keepdims=True)
    return x * lax.rsqrt(ms + NORM_EPS) * gain


def _rope_tables(pos, inv_freq, half):
    lane = lax.broadcasted_iota(jnp.int32, (1, LANES), 1)
    ang = pos * inv_freq
    sign = jnp.where(lane < half, -1.0, 1.0).astype(F32)
    return jnp.cos(ang), jnp.sin(ang) * sign


def _rope(r, cos, sin, half):
    lane = lax.broadcasted_iota(jnp.int32, (1, LANES), 1)
    partner = jnp.where(lane < half,
                        pltpu.roll(r, LANES - half, 1),
                        pltpu.roll(r, half, 1))
    return r * cos + partner * sin


def _inproj_kernel(x_ref, pos_ref, g_ref, w_ref, invfa_ref, invfb_ref, gq_ref, gkv_ref,
                   aq_ref, ak_ref, av_ref, cq_ref, ckv_ref, kr_ref, *, a_width, q_rank, kv_rank):
    h = _rms(x_ref[...], g_ref[...]).astype(BF16)
    pos = pos_ref[...]
    cos_a, sin_a = _rope_tables(pos, invfa_ref[...], ROT_DIM // 2)
    cos_b, sin_b = _rope_tables(pos, invfb_ref[...], QK_ROPE_DIM // 2)
    n_heads = a_width // HEAD_DIM
    q_scale = HEAD_DIM ** -0.5

    r = jnp.dot(h, w_ref[:, 0:a_width], preferred_element_type=F32)
    for hd in range(n_heads):
        sl = slice(hd * HEAD_DIM, (hd + 1) * HEAD_DIM)
        aq_ref[:, sl] = _rope(r[:, sl], cos_a, sin_a, ROT_DIM // 2) * q_scale
    r = jnp.dot(h, w_ref[:, a_width:2 * a_width], preferred_element_type=F32)
    for hd in range(n_heads):
        sl = slice(hd * HEAD_DIM, (hd + 1) * HEAD_DIM)
        ak_ref[:, sl] = _rope(r[:, sl], cos_a, sin_a, ROT_DIM // 2)
    av_ref[...] = jnp.dot(h, w_ref[:, 2 * a_width:3 * a_width], preferred_element_type=F32)

    c0 = 3 * a_width
    r = jnp.dot(h, w_ref[:, c0:], preferred_element_type=F32)
    cq_ref[...] = _rms(r[:, 0:q_rank], gq_ref[...]).astype(BF16)
    ckv_ref[...] = _rms(r[:, q_rank:q_rank + kv_rank], gkv_ref[...]).astype(BF16)
    kr = r[:, q_rank + kv_rank:q_rank + kv_rank + LANES]
    kr_ref[...] = _rope(kr, cos_b, sin_b, QK_ROPE_DIM // 2).astype(BF16)


def _inproj(x2d, pos, g_pre, w_in, invfa, invfb, gq, gkv, *, a_width, q_rank, kv_rank, tm):
    T, D = x2d.shape
    ncols = w_in.shape[1]
    const = lambda i: (0, 0)
    row = lambda i: (i, 0)
    kern = functools.partial(_inproj_kernel, a_width=a_width, q_rank=q_rank, kv_rank=kv_rank)
    return pl.pallas_call(
        kern,
        grid=(T // tm,),
        in_specs=[
            pl.BlockSpec((tm, D), row),
            pl.BlockSpec((tm, 1), row),
            pl.BlockSpec((1, D), const),
            pl.BlockSpec((D, ncols), const, pipeline_mode=pl.Buffered(1)),
            pl.BlockSpec((1, LANES), const),
            pl.BlockSpec((1, LANES), const),
            pl.BlockSpec((1, q_rank), const),
            pl.BlockSpec((1, kv_rank), const),
        ],
        out_specs=[
            pl.BlockSpec((tm, a_width), row),
            pl.BlockSpec((tm, a_width), row),
            pl.BlockSpec((tm, a_width), row),
            pl.BlockSpec((tm, q_rank), row),
            pl.BlockSpec((tm, kv_rank), row),
            pl.BlockSpec((tm, LANES), row),
        ],
        out_shape=[
            jax.ShapeDtypeStruct((T, a_width), F32),
            jax.ShapeDtypeStruct((T, a_width), F32),
            jax.ShapeDtypeStruct((T, a_width), F32),
            jax.ShapeDtypeStruct((T, q_rank), BF16),
            jax.ShapeDtypeStruct((T, kv_rank), BF16),
            jax.ShapeDtypeStruct((T, LANES), BF16),
        ],
        compiler_params=pltpu.CompilerParams(
            dimension_semantics=("parallel",), vmem_limit_bytes=VMEM_LIMIT),
        name="inproj",
    )(x2d, pos, g_pre, w_in, invfa, invfb, gq, gkv)


def _mla_up_kernel(cq_ref, ckv_ref, kr_ref, pos_ref, invfb_ref, wq_ref, wk_ref, wv_ref,
                   q_ref, k_ref, v_ref, *, n_heads):
    cos_b, sin_b = _rope_tables(pos_ref[...], invfb_ref[...], QK_ROPE_DIM // 2)
    scale = (QK_NOPE_DIM + QK_ROPE_DIM) ** -0.5
    rq = jnp.dot(cq_ref[...], wq_ref[...], preferred_element_type=F32)
    ckv = ckv_ref[...]
    rk = jnp.dot(ckv, wk_ref[...], preferred_element_type=F32)
    kr = kr_ref[...]
    for hd in range(n_heads):
        base = hd * MLA_QK_PAD
        q_ref[:, base:base + LANES] = (rq[:, base:base + LANES] * scale).astype(BF16)
        q_rot = _rope(rq[:, base + LANES:base + 2 * LANES], cos_b, sin_b, QK_ROPE_DIM // 2)
        q_ref[:, base + LANES:base + 2 * LANES] = (q_rot * scale).astype(BF16)
        k_ref[:, base:base + LANES] = rk[:, hd * LANES:(hd + 1) * LANES].astype(BF16)
        k_ref[:, base + LANES:base + 2 * LANES] = kr
    v_ref[...] = jnp.dot(ckv, wv_ref[...], preferred_element_type=F32).astype(BF16)


def _mla_up(cq, ckv, kr, pos, invfb, wq, wk, wv, *, n_heads, tm):
    T = cq.shape[0]
    const = lambda i: (0, 0)
    row = lambda i: (i, 0)
    qk_w = n_heads * MLA_QK_PAD
    v_w = n_heads * V_HEAD_DIM
    return pl.pallas_call(
        functools.partial(_mla_up_kernel, n_heads=n_heads),
        grid=(T // tm,),
        in_specs=[
            pl.BlockSpec((tm, cq.shape[1]), row),
            pl.BlockSpec((tm, ckv.shape[1]), row),
            pl.BlockSpec((tm, LANES), row),
            pl.BlockSpec((tm, 1), row),
            pl.BlockSpec((1, LANES), const),
            pl.BlockSpec(wq.shape, const),
            pl.BlockSpec(wk.shape, const),
            pl.BlockSpec(wv.shape, const),
        ],
        out_specs=[
            pl.BlockSpec((tm, qk_w), row),
            pl.BlockSpec((tm, qk_w), row),
            pl.BlockSpec((tm, v_w), row),
        ],
        out_shape=[
            jax.ShapeDtypeStruct((T, qk_w), BF16),
            jax.ShapeDtypeStruct((T, qk_w), BF16),
            jax.ShapeDtypeStruct((T, v_w), BF16),
        ],
        compiler_params=pltpu.CompilerParams(
            dimension_semantics=("parallel",), vmem_limit_bytes=VMEM_LIMIT),
        name="mla_up",
    )(cq, ckv, kr, pos, invfb, wq, wk, wv)


DSWA_GROUP = 4


def _dswa_kernel(q_ref, k_ref, v_ref, o_ref, m_sc, l_sc, acc_sc, *, seq):
    qi = lax.broadcasted_iota(jnp.int32, (Q_BLOCK, 2 * Q_BLOCK), 0)
    kj = lax.broadcasted_iota(jnp.int32, (Q_BLOCK, 2 * Q_BLOCK), 1)
    bias_band = jnp.where(kj >= qi, jnp.where(kj <= qi + Q_BLOCK, 0.0, NEG_INF), NEG_INF).astype(F32)
    bias_first = jnp.where(kj <= qi, 0.0, NEG_INF).astype(F32)

    def rows(start, size, d):
        return pl.ds(start, size) if d == 1 else pl.ds(start, size, stride=d)

    def tile(d, r, n, first_cfg):
        q_rows = rows(n * (Q_BLOCK * d) + r, Q_BLOCK, d)
        k_rows = rows(jnp.maximum(n - 1, 0) * (Q_BLOCK * d) + r, 2 * Q_BLOCK, d)
        q = q_ref[q_rows, :].astype(BF16)
        k = k_ref[k_rows, :].astype(BF16)
        v = v_ref[k_rows, :].astype(BF16)
        s = lax.dot_general(q, k, (((1,), (1,)), ((), ())), preferred_element_type=F32)
        s = s + jnp.where(n == 0, bias_first, bias_band)
        s0, s1 = s[:, :Q_BLOCK], s[:, Q_BLOCK:]
        m_cur = jnp.maximum(jnp.max(s0, axis=1, keepdims=True), jnp.max(s1, axis=1, keepdims=True))
        if first_cfg:
            m_new = jnp.broadcast_to(m_cur, (Q_BLOCK, LANES))
        else:
            m_prev = m_sc[q_rows, :]
            m_new = jnp.maximum(m_prev, m_cur)
            alpha = jnp.exp(m_prev - m_new)
        p0 = jnp.exp(s0 - m_new)
        p1 = jnp.exp(s1 - m_new)
        l_cur = jnp.sum(p0, axis=1, keepdims=True) + jnp.sum(p1, axis=1, keepdims=True)
        p = jnp.concatenate([p0, p1], axis=1).astype(BF16)
        pv = jnp.dot(p, v, preferred_element_type=F32)
        if first_cfg:
            l_new = jnp.broadcast_to(l_cur, (Q_BLOCK, LANES))
            acc_new = pv
        else:
            l_new = alpha * l_sc[q_rows, :] + l_cur
            acc_new = alpha * acc_sc[q_rows, :] + pv
        return q_rows, m_new, l_new, acc_new

    def run_config(d, first_cfg):
        n_blocks = seq // (Q_BLOCK * d)
        n_tiles = d * n_blocks
        def group(g, carry):
            results = []
            for j in range(DSWA_GROUP):
                t = g * DSWA_GROUP + j
                results.append(tile(d, t % d, t // d, first_cfg))
            for q_rows, m_new, l_new, acc_new in results:
                m_sc[q_rows, :] = m_new
                l_sc[q_rows, :] = l_new
                acc_sc[q_rows, :] = acc_new
            return carry
        lax.fori_loop(0, n_tiles // DSWA_GROUP, group, 0)

    for ci, d in enumerate(DILATIONS):
        run_config(d, ci == 0)
    o_ref[...] = (acc_sc[...] / l_sc[...]).astype(o_ref.dtype)


def _dswa(aq, ak, av, *, n_heads):
    B, S, _ = aq.shape
    assert S % (Q_BLOCK * max(DILATIONS) * DSWA_GROUP // max(DILATIONS)) == 0
    assert all(S % (Q_BLOCK * d) == 0 for d in DILATIONS)
    spec = pl.BlockSpec((None, S, HEAD_DIM), lambda b, h: (b, 0, h))
    return pl.pallas_call(
        functools.partial(_dswa_kernel, seq=S),
        grid=(B, n_heads),
        in_specs=[spec, spec, spec],
        out_specs=spec,
        out_shape=jax.ShapeDtypeStruct(aq.shape, BF16),
        scratch_shapes=[pltpu.VMEM((S, LANES), F32)] * 3,
        compiler_params=pltpu.CompilerParams(
            dimension_semantics=("parallel", "parallel"), vmem_limit_bytes=VMEM_LIMIT),
        name="dswa",
    )(aq, ak, av)


def _mla_attn_kernel(q_ref, k_ref, v_ref, o_ref, *, tq, tk):
    qi = pl.program_id(2)
    q = q_ref[...]
    row = qi * tq + lax.broadcasted_iota(jnp.int32, (tq, tk), 0)
    col = lax.broadcasted_iota(jnp.int32, (tq, tk), 1)

    def step(j, carry, masked):
        m, l, acc = carry
        start = pl.multiple_of(j * tk, tk)
        k = k_ref[pl.ds(start, tk), :]
        v = v_ref[pl.ds(start, tk), :]
        s = lax.dot_general(q, k, (((1,), (1,)), ((), ())), preferred_element_type=F32)
        if masked:
            s = jnp.where(col + start <= row, s, NEG_INF)
        m_new = jnp.maximum(m, jnp.max(s, axis=1, keepdims=True))
        alpha = jnp.exp(m - m_new)
        p = jnp.exp(s - m_new)
        l_new = alpha * l + jnp.sum(p, axis=1, keepdims=True)
        acc_new = alpha * acc + jnp.dot(p.astype(BF16), v, preferred_element_type=F32)
        return m_new, l_new, acc_new

    init = (jnp.full((tq, 1), NEG_INF, F32), jnp.zeros((tq, 1), F32),
            jnp.zeros((tq, V_HEAD_DIM), F32))
    n_full = qi * (tq // tk)
    carry = lax.fori_loop(0, n_full, lambda j, c: step(j, c, False), init)
    for jj in range(tq // tk):
        carry = step(n_full + jj, carry, True)
    _, l, acc = carry
    o_ref[...] = (acc / l).astype(o_ref.dtype)


def _mla_attn(q, k, v, *, n_heads, tq, tk):
    B, S, _ = q.shape
    return pl.pallas_call(
        functools.partial(_mla_attn_kernel, tq=tq, tk=tk),
        grid=(B, n_heads, S // tq),
        in_specs=[
            pl.BlockSpec((None, tq, MLA_QK_PAD), lambda b, h, i: (b, i, h)),
            pl.BlockSpec((None, S, MLA_QK_PAD), lambda b, h, i: (b, 0, h)),
            pl.BlockSpec((None, S, V_HEAD_DIM), lambda b, h, i: (b, 0, h)),
        ],
        out_specs=pl.BlockSpec((None, tq, V_HEAD_DIM), lambda b, h, i: (b, i, h)),
        out_shape=jax.ShapeDtypeStruct((B, S, n_heads * V_HEAD_DIM), BF16),
        compiler_params=pltpu.CompilerParams(
            dimension_semantics=("parallel", "parallel", "arbitrary"),
            vmem_limit_bytes=VMEM_LIMIT),
        name="mla_attn",
    )(q, k, v)


def _outproj_kernel(a_ref, b_ref, x_ref, w_ref, g_ref, o_ref, *, a_width):
    y = jnp.dot(a_ref[...], w_ref[0:a_width, :], preferred_element_type=F32)
    y = y + jnp.dot(b_ref[...], w_ref[a_width:, :], preferred_element_type=F32)
    o_ref[...] = x_ref[...] + _rms(y, g_ref[...])


def _outproj(a, b, x2d, w_out, g_post, *, tm):
    T, D = x2d.shape
    const = lambda i: (0, 0)
    row = lambda i: (i, 0)
    return pl.pallas_call(
        functools.partial(_outproj_kernel, a_width=a.shape[1]),
        grid=(T // tm,),
        in_specs=[
            pl.BlockSpec((tm, a.shape[1]), row),
            pl.BlockSpec((tm, b.shape[1]), row),
            pl.BlockSpec((tm, D), row),
            pl.BlockSpec(w_out.shape, const, pipeline_mode=pl.Buffered(1)),
            pl.BlockSpec((1, D), const),
        ],
        out_specs=pl.BlockSpec((tm, D), row),
        out_shape=jax.ShapeDtypeStruct((T, D), F32),
        compiler_params=pltpu.CompilerParams(
            dimension_semantics=("parallel",), vmem_limit_bytes=VMEM_LIMIT),
        name="outproj",
    )(a, b, x2d, w_out, g_post)


def _mlp_kernel(x_ref, gpre_ref, gpost_ref, wu_ref, wd_ref, o_ref, h_sc, acc_sc):
    f = pl.program_id(1)

    @pl.when(f == 0)
    def _():
        h_sc[...] = _rms(x_ref[...], gpre_ref[...]).astype(BF16)

    u = jnp.dot(h_sc[...], wu_ref[...], preferred_element_type=F32)
    u = jnp.square(jnp.maximum(u, 0.0)).astype(BF16)
    part = jnp.dot(u, wd_ref[...], preferred_element_type=F32)

    @pl.when(f == 0)
    def _():
        acc_sc[...] = part

    @pl.when(f > 0)
    def _():
        acc_sc[...] += part

    @pl.when(f == pl.num_programs(1) - 1)
    def _():
        o_ref[...] = x_ref[...] + _rms(acc_sc[...], gpost_ref[...])


def _mlp(x2d, g_pre, g_post, w_up, w_down, *, tm, tf):
    T, D = x2d.shape
    F = w_up.shape[1]
    return pl.pallas_call(
        _mlp_kernel,
        grid=(T // tm, F // tf),
        in_specs=[
            pl.BlockSpec((tm, D), lambda i, f: (i, 0)),
            pl.BlockSpec((1, D), lambda i, f: (0, 0)),
            pl.BlockSpec((1, D), lambda i, f: (0, 0)),
            pl.BlockSpec((D, tf), lambda i, f: (0, f)),
            pl.BlockSpec((tf, D), lambda i, f: (f, 0)),
        ],
        out_specs=pl.BlockSpec((tm, D), lambda i, f: (i, 0)),
        out_shape=jax.ShapeDtypeStruct((T, D), F32),
        scratch_shapes=[pltpu.VMEM((tm, D), BF16), pltpu.VMEM((tm, D), F32)],
        compiler_params=pltpu.CompilerParams(
            dimension_semantics=("parallel", "arbitrary"), vmem_limit_bytes=VMEM_LIMIT),
        name="mlp",
    )(x2d, g_pre, g_post, w_up, w_down)


def _inv_freq_lanes(rot_dim):
    inv = ROPE_THETA ** (-jnp.arange(0, rot_dim, 2, dtype=F32) / rot_dim)
    lanes = jnp.concatenate([inv, inv, jnp.zeros((LANES - rot_dim,), F32)])
    return lanes.reshape(1, LANES)


def _layer(x2d, pos, B, S, norm_attn_pre, norm_attn_post, w_in, q_latent_norm, kv_latent_norm,
           w_uq, w_ukv, w_out, norm_mlp_pre, norm_mlp_post, w_up, w_down):
    D = x2d.shape[1]
    q_rank = q_latent_norm.shape[0]
    kv_rank = kv_latent_norm.shape[0]
    n_mla = w_uq.shape[1] // (QK_NOPE_DIM + QK_ROPE_DIM)
    a_width = (w_in.shape[1] - q_rank - kv_rank - QK_ROPE_DIM) // 3
    n_a = a_width // HEAD_DIM

    w_in_p = jnp.pad(w_in, ((0, 0), (0, LANES - QK_ROPE_DIM))).astype(BF16)
    wq3 = w_uq.reshape(q_rank, n_mla, QK_NOPE_DIM + QK_ROPE_DIM)
    wq_p = jnp.pad(wq3, ((0, 0), (0, 0), (0, MLA_QK_PAD - QK_NOPE_DIM - QK_ROPE_DIM)))
    wq_p = wq_p.reshape(q_rank, n_mla * MLA_QK_PAD).astype(BF16)
    wkv3 = w_ukv.reshape(kv_rank, n_mla, QK_NOPE_DIM + V_HEAD_DIM)
    wk_p = wkv3[:, :, :QK_NOPE_DIM].reshape(kv_rank, n_mla * QK_NOPE_DIM).astype(BF16)
    wv_p = wkv3[:, :, QK_NOPE_DIM:].reshape(kv_rank, n_mla * V_HEAD_DIM).astype(BF16)

    invfa = _inv_freq_lanes(ROT_DIM)
    invfb = _inv_freq_lanes(QK_ROPE_DIM)
    row = lambda g: g.reshape(1, -1)

    aq, ak, av, cq, ckv, kr = _inproj(
        x2d, pos, row(norm_attn_pre), w_in_p, invfa, invfb, row(q_latent_norm),
        row(kv_latent_norm), a_width=a_width, q_rank=q_rank, kv_rank=kv_rank, tm=512)
    q_b, k_b, v_b = _mla_up(cq, ckv, kr, pos, invfb, wq_p, wk_p, wv_p, n_heads=n_mla, tm=512)

    a_out = _dswa(aq.reshape(B, S, a_width), ak.reshape(B, S, a_width),
                  av.reshape(B, S, a_width), n_heads=n_a)
    b_out = _mla_attn(q_b.reshape(B, S, -1), k_b.reshape(B, S, -1), v_b.reshape(B, S, -1),
                      n_heads=n_mla, tq=512, tk=512)

    x2d = _outproj(a_out.reshape(B * S, -1), b_out.reshape(B * S, -1), x2d,
                   w_out.astype(BF16), row(norm_attn_post), tm=512)
    return _mlp(x2d, row(norm_mlp_pre), row(norm_mlp_post), w_up.astype(BF16),
                w_down.astype(BF16), tm=512, tf=1024)


def kernel(x, positions, norm_attn_pre, norm_attn_post, w_in, q_latent_norm, kv_latent_norm,
           w_uq, w_ukv, w_out, norm_mlp_pre, norm_mlp_post, w_up, w_down):
    B, S, D = x.shape
    x2d = x.reshape(B * S, D)
    pos = positions.astype(F32).reshape(B * S, 1)
    for layer in range(w_in.shape[0]):
        x2d = _layer(x2d, pos, B, S, norm_attn_pre[layer], norm_attn_post[layer], w_in[layer],
                     q_latent_norm[layer], kv_latent_norm[layer], w_uq[layer], w_ukv[layer],
                     w_out[layer], norm_mlp_pre[layer], norm_mlp_post[layer], w_up[layer],
                     w_down[layer])
    return x2d.reshape(B, S, D)
```

```python
import functools

import jax
import jax.numpy as jnp
from jax import lax
from jax.experimental import pallas as pl
from jax.experimental.pallas import tpu as pltpu

F32 = jnp.float32
BF16 = jnp.bfloat16

LANES = 128
HEAD_DIM = 128
ROT_DIM = HEAD_DIM // 4
ROPE_THETA = 500000.0
QK_NOPE_DIM = 128
QK_ROPE_DIM = 64
V_HEAD_DIM = 128
MLA_QK_PAD = 256
Q_BLOCK = 128
DILATIONS = (1, 4, 16)
NORM_EPS = 1e-6
NEG_INF = -1e30
LOG2E = 1.4426950408889634
VMEM_LIMIT = 56 * 1024 * 1024


def _rms(x, gain):
    ms = jnp.mean(x * x, axis=-1, keepdims=True)
    return x * lax.rsqrt(ms + NORM_EPS) * gain


def _rope_tables(pos, inv_freq, half):
    lane = lax.broadcasted_iota(jnp.int32, (1, LANES), 1)
    ang = pos * inv_freq
    sign = jnp.where(lane < half, -1.0, 1.0).astype(F32)
    return jnp.cos(ang), jnp.sin(ang) * sign


def _rope(r, cos, sin, half):
    lane = lax.broadcasted_iota(jnp.int32, (1, LANES), 1)
    partner = jnp.where(lane < half,
                        pltpu.roll(r, LANES - half, 1),
                        pltpu.roll(r, half, 1))
    return r * cos + partner * sin


def _inproj_kernel(x_ref, pos_ref, g_ref, w_ref, invfa_ref, invfb_ref, gq_ref, gkv_ref,
                   aq_ref, ak_ref, av_ref, cq_ref, ckv_ref, kr_ref, *, a_width, q_rank, kv_rank):
    h = _rms(x_ref[...], g_ref[...]).astype(BF16)
    pos = pos_ref[...]
    cos_a, sin_a = _rope_tables(pos, invfa_ref[...], ROT_DIM // 2)
    cos_b, sin_b = _rope_tables(pos, invfb_ref[...], QK_ROPE_DIM // 2)
    n_heads = a_width // HEAD_DIM
    q_scale = HEAD_DIM ** -0.5 * LOG2E

    r = jnp.dot(h, w_ref[:, 0:a_width], preferred_element_type=F32)
    for hd in range(n_heads):
        sl = slice(hd * HEAD_DIM, (hd + 1) * HEAD_DIM)
        aq_ref[:, sl] = _rope(r[:, sl], cos_a, sin_a, ROT_DIM // 2) * q_scale
    r = jnp.dot(h, w_ref[:, a_width:2 * a_width], preferred_element_type=F32)
    for hd in range(n_heads):
        sl = slice(hd * HEAD_DIM, (hd + 1) * HEAD_DIM)
        ak_ref[:, sl] = _rope(r[:, sl], cos_a, sin_a, ROT_DIM // 2)
    av_ref[...] = jnp.dot(h, w_ref[:, 2 * a_width:3 * a_width], preferred_element_type=F32)

    c0 = 3 * a_width
    r = jnp.dot(h, w_ref[:, c0:], preferred_element_type=F32)
    cq_ref[...] = _rms(r[:, 0:q_rank], gq_ref[...]).astype(BF16)
    ckv_ref[...] = _rms(r[:, q_rank:q_rank + kv_rank], gkv_ref[...]).astype(BF16)
    kr = r[:, q_rank + kv_rank:q_rank + kv_rank + LANES]
    kr_ref[...] = _rope(kr, cos_b, sin_b, QK_ROPE_DIM // 2).astype(BF16)


def _inproj(x2d, pos, g_pre, w_in, invfa, invfb, gq, gkv, *, a_width, q_rank, kv_rank, tm):
    T, D = x2d.shape
    ncols = w_in.shape[1]
    const = lambda i: (0, 0)
    row = lambda i: (i, 0)
    kern = functools.partial(_inproj_kernel, a_width=a_width, q_rank=q_rank, kv_rank=kv_rank)
    return pl.pallas_call(
        kern,
        grid=(T // tm,),
        in_specs=[
            pl.BlockSpec((tm, D), row),
            pl.BlockSpec((tm, 1), row),
            pl.BlockSpec((1, D), const),
            pl.BlockSpec((D, ncols), const, pipeline_mode=pl.Buffered(1)),
            pl.BlockSpec((1, LANES), const),
            pl.BlockSpec((1, LANES), const),
            pl.BlockSpec((1, q_rank), const),
            pl.BlockSpec((1, kv_rank), const),
        ],
        out_specs=[
            pl.BlockSpec((tm, a_width), row),
            pl.BlockSpec((tm, a_width), row),
            pl.BlockSpec((tm, a_width), row),
            pl.BlockSpec((tm, q_rank), row),
            pl.BlockSpec((tm, kv_rank), row),
            pl.BlockSpec((tm, LANES), row),
        ],
        out_shape=[
            jax.ShapeDtypeStruct((T, a_width), F32),
            jax.ShapeDtypeStruct((T, a_width), F32),
            jax.ShapeDtypeStruct((T, a_width), F32),
            jax.ShapeDtypeStruct((T, q_rank), BF16),
            jax.ShapeDtypeStruct((T, kv_rank), BF16),
            jax.ShapeDtypeStruct((T, LANES), BF16),
        ],
        compiler_params=pltpu.CompilerParams(
            dimension_semantics=("parallel",), vmem_limit_bytes=VMEM_LIMIT),
        name="inproj",
    )(x2d, pos, g_pre, w_in, invfa, invfb, gq, gkv)


def _mla_up_kernel(cq_ref, ckv_ref, kr_ref, pos_ref, invfb_ref, wq_ref, wk_ref, wv_ref,
                   q_ref, k_ref, v_ref, *, n_heads):
    cos_b, sin_b = _rope_tables(pos_ref[...], invfb_ref[...], QK_ROPE_DIM // 2)
    scale = (QK_NOPE_DIM + QK_ROPE_DIM) ** -0.5 * LOG2E
    rq = jnp.dot(cq_ref[...], wq_ref[...], preferred_element_type=F32)
    ckv = ckv_ref[...]
    rk = jnp.dot(ckv, wk_ref[...], preferred_element_type=F32)
    kr = kr_ref[...]
    for hd in range(n_heads):
        base = hd * MLA_QK_PAD
        q_ref[:, base:base + LANES] = (rq[:, base:base + LANES] * scale).astype(BF16)
        q_rot = _rope(rq[:, base + LANES:base + 2 * LANES], cos_b, sin_b, QK_ROPE_DIM // 2)
        q_ref[:, base + LANES:base + 2 * LANES] = (q_rot * scale).astype(BF16)
        k_ref[:, base:base + LANES] = rk[:, hd * LANES:(hd + 1) * LANES].astype(BF16)
        k_ref[:, base + LANES:base + 2 * LANES] = kr
    v_ref[...] = jnp.dot(ckv, wv_ref[...], preferred_element_type=F32).astype(BF16)


def _mla_up(cq, ckv, kr, pos, invfb, wq, wk, wv, *, n_heads, tm):
    T = cq.shape[0]
    const = lambda i: (0, 0)
    row = lambda i: (i, 0)
    qk_w = n_heads * MLA_QK_PAD
    v_w = n_heads * V_HEAD_DIM
    return pl.pallas_call(
        functools.partial(_mla_up_kernel, n_heads=n_heads),
        grid=(T // tm,),
        in_specs=[
            pl.BlockSpec((tm, cq.shape[1]), row),
            pl.BlockSpec((tm, ckv.shape[1]), row),
            pl.BlockSpec((tm, LANES), row),
            pl.BlockSpec((tm, 1), row),
            pl.BlockSpec((1, LANES), const),
            pl.BlockSpec(wq.shape, const),
            pl.BlockSpec(wk.shape, const),
            pl.BlockSpec(wv.shape, const),
        ],
        out_specs=[
            pl.BlockSpec((tm, qk_w), row),
            pl.BlockSpec((tm, qk_w), row),
            pl.BlockSpec((tm, v_w), row),
        ],
        out_shape=[
            jax.ShapeDtypeStruct((T, qk_w), BF16),
            jax.ShapeDtypeStruct((T, qk_w), BF16),
            jax.ShapeDtypeStruct((T, v_w), BF16),
        ],
        compiler_params=pltpu.CompilerParams(
            dimension_semantics=("parallel",), vmem_limit_bytes=VMEM_LIMIT),
        name="mla_up",
    )(cq, ckv, kr, pos, invfb, wq, wk, wv)


DSWA_GROUP = 4
MAX_DIL = 16
NT_DIMS = (((1,), (1,)), ((), ()))


def _aligned(x, multiple):
    return x if isinstance(x, int) else pl.multiple_of(x, multiple)


def _softmax_block(q, k, v, bias, prev):
    nk = k.shape[0]
    s = lax.dot_general(q, k, NT_DIMS, preferred_element_type=F32)
    blocks = [s[:, i * LANES:(i + 1) * LANES] for i in range(nk // LANES)]
    if bias is not None:
        nb = bias.shape[1] // LANES
        blocks[-nb:] = [b + bias[:, i * LANES:(i + 1) * LANES]
                        for i, b in enumerate(blocks[-nb:])]
    m_cur = jnp.max(functools.reduce(jnp.maximum, blocks), axis=1, keepdims=True)
    if prev is None:
        m_new = jnp.broadcast_to(m_cur, (q.shape[0], LANES))
    else:
        m_prev, l_prev, acc_prev = prev
        m_new = jnp.maximum(m_prev, m_cur)
        alpha = jnp.exp2(m_prev - m_new)
    p = jnp.concatenate([jnp.exp2(b - m_new) for b in blocks], axis=1).astype(BF16)
    v_aug = jnp.concatenate([v, jnp.ones((nk, LANES), BF16)], axis=1)
    pv = jnp.dot(p, v_aug, preferred_element_type=F32)
    acc_cur, l_cur = pv[:, :LANES], pv[:, LANES:]
    if prev is None:
        return m_new, l_cur, acc_cur
    return m_new, alpha * l_prev + l_cur, alpha * acc_prev + acc_cur


def _dswa_kernel(q_ref, k_ref, v_ref, o_ref, xq, xk, xv, mx, lx, ax, mn, ln, an, x4, *, seq):
    U = seq // MAX_DIL
    QB = Q_BLOCK
    NEG = NEG_INF

    def band_bias(dist):
        return jnp.where(dist >= 0, jnp.where(dist <= QB, 0.0, NEG), NEG).astype(F32)

    i1 = lax.broadcasted_iota(jnp.int32, (QB, QB), 0)
    j1 = lax.broadcasted_iota(jnp.int32, (QB, QB), 1)
    i2 = lax.broadcasted_iota(jnp.int32, (QB, 2 * QB), 0)
    j2 = lax.broadcasted_iota(jnp.int32, (QB, 2 * QB), 1)
    bias_first = band_bias(i1 - j1)
    bias_band = band_bias(i2 + QB - j2)
    qc = QB // 4
    lo = lambda x, n: x & (n - 1)
    hi = lambda x, n: x >> (n.bit_length() - 1)
    d4_first = band_bias(4 * (lo(i1, qc) - lo(j1, qc)) + hi(i1, qc) - hi(j1, qc))
    d4_band = band_bias(4 * (lo(i2, qc) - lo(j2, 2 * qc) + qc) + hi(i2, qc) - hi(j2, 2 * qc))

    for src, dst in ((q_ref, xq), (k_ref, xk), (v_ref, xv)):
        for r4 in range(4):
            x4[r4] = src[pl.ds(r4, 4 * U, stride=4), :]
        for r in range(MAX_DIL):
            dst[r] = x4[r % 4, pl.ds(r // 4, U, stride=4), :].astype(BF16)

    def d16_group(g, carry):
        out = []
        for j in range(DSWA_GROUP):
            r = g * DSWA_GROUP + j
            out.append((r, 0, _softmax_block(xq[r, 0:QB, :], xk[r, 0:QB, :], xv[r, 0:QB, :],
                                             bias_first, None)))
            for n in range(1, U // QB):
                ks = slice((n - 1) * QB, (n + 1) * QB)
                out.append((r, n, _softmax_block(xq[r, n * QB:(n + 1) * QB, :], xk[r, ks, :],
                                                 xv[r, ks, :], bias_band, None)))
        for r, n, (m_new, l_new, acc_new) in out:
            qs = slice(n * QB, (n + 1) * QB)
            mx[r, qs, :] = m_new
            lx[r, qs, :] = l_new
            ax[r, qs, :] = acc_new
        return carry
    lax.fori_loop(0, MAX_DIL // DSWA_GROUP, d16_group, 0)

    def d4_tile(r4, a, first):
        def gather(ref, start, size):
            return jnp.concatenate([ref[4 * qq + r4, pl.ds(start, size), :] for qq in range(4)],
                                   axis=0)
        q = gather(xq, a, qc)
        prev = (gather(mx, a, qc), gather(lx, a, qc), gather(ax, a, qc))
        if first:
            k, v, bias = gather(xk, a, qc), gather(xv, a, qc), d4_first
        else:
            k, v, bias = gather(xk, a - qc, 2 * qc), gather(xv, a - qc, 2 * qc), d4_band
        return _softmax_block(q, k, v, bias, prev)

    def d4_store(r4, a, res):
        for ref, val in zip((mx, lx, ax), res):
            for qq in range(4):
                ref[4 * qq + r4, pl.ds(a, qc), :] = val[qq * qc:(qq + 1) * qc]

    def d4_group(ai, first):
        a = _aligned(ai * qc, qc)
        res = [d4_tile(r4, a, first) for r4 in range(4)]
        for r4 in range(4):
            d4_store(r4, a, res[r4])

    d4_group(0, True)
    lax.fori_loop(1, U // qc, lambda ai, c: (d4_group(ai, False), c)[1], 0)

    for src, dst in ((mx, mn), (lx, ln), (ax, an)):
        for r in range(MAX_DIL):
            x4[r % 4, pl.ds(r // 4, U, stride=4), :] = src[r]
        for r4 in range(4):
            dst[pl.ds(r4, 4 * U, stride=4), :] = x4[r4]

    def d1_tile(n, first):
        qs = pl.ds(_aligned(n * QB, QB), QB)
        if first:
            ks, bias = qs, bias_first
        else:
            ks, bias = pl.ds(_aligned((n - 1) * QB, QB), 2 * QB), bias_band
        prev = (mn[qs, :], ln[qs, :], an[qs, :])
        _, l_new, acc_new = _softmax_block(q_ref[qs, :].astype(BF16), k_ref[ks, :].astype(BF16),
                                           v_ref[ks, :].astype(BF16), bias, prev)
        return qs, (acc_new / l_new).astype(o_ref.dtype)

    def d1_group(g, first_group):
        res = [d1_tile(g * DSWA_GROUP + j, first_group and j == 0) for j in range(DSWA_GROUP)]
        for qs, val in res:
            o_ref[qs, :] = val

    d1_group(0, True)
    lax.fori_loop(1, seq // QB // DSWA_GROUP, lambda g, c: (d1_group(g, False), c)[1], 0)


def _dswa(aq, ak, av, *, n_heads):
    B, S, _ = aq.shape
    U = S // MAX_DIL
    assert DILATIONS == (1, 4, MAX_DIL) and MAX_DIL % DSWA_GROUP == 0
    assert U % Q_BLOCK == 0 and (S // Q_BLOCK) % DSWA_GROUP == 0
    spec = pl.BlockSpec((None, S, HEAD_DIM), lambda b, h: (b, 0, h))
    x16 = lambda dt: pltpu.VMEM((MAX_DIL, U, LANES), dt)
    return pl.pallas_call(
        functools.partial(_dswa_kernel, seq=S),
        grid=(B, n_heads),
        in_specs=[spec, spec, spec],
        out_specs=spec,
        out_shape=jax.ShapeDtypeStruct(aq.shape, BF16),
        scratch_shapes=([x16(BF16)] * 3 + [x16(F32)] * 3 + [pltpu.VMEM((S, LANES), F32)] * 3
                        + [pltpu.VMEM((4, S // 4, LANES), F32)]),
        compiler_params=pltpu.CompilerParams(
            dimension_semantics=("parallel", "parallel"), vmem_limit_bytes=VMEM_LIMIT),
        name="dswa",
    )(aq, ak, av)


MLA_CHAIN = 256


def _mla_attn_kernel(q_ref, k_ref, v_ref, o_ref, m_sc, l_sc, acc_sc, *, tq, tk):
    qi = pl.program_id(2)
    cq = MLA_CHAIN
    n_chains = tq // cq
    ri = lax.broadcasted_iota(jnp.int32, (cq, cq), 0)
    ci = lax.broadcasted_iota(jnp.int32, (cq, cq), 1)
    diag_bias = jnp.where(ci <= ri, 0.0, NEG_INF).astype(F32)

    m_sc[...] = jnp.full(m_sc.shape, NEG_INF, F32)
    l_sc[...] = jnp.zeros(l_sc.shape, F32)
    acc_sc[...] = jnp.zeros(acc_sc.shape, F32)

    def step(c, k_start, nk, bias):
        rows = slice(c * cq, (c + 1) * cq)
        ks = pl.ds(k_start, nk)
        prev = (m_sc[rows, :], l_sc[rows, :], acc_sc[rows, :])
        m_new, l_new, acc_new = _softmax_block(q_ref[rows, :], k_ref[ks, :], v_ref[ks, :],
                                               bias, prev)
        m_sc[rows, :] = m_new
        l_sc[rows, :] = l_new
        acc_sc[rows, :] = acc_new

    blocks_per_tile = tq // tk

    def full_blocks(t, carry):
        for jj in range(blocks_per_tile):
            k_start = pl.multiple_of((t * blocks_per_tile + jj) * tk, tk)
            for c in range(n_chains):
                step(c, k_start, tk, None)
        return carry
    lax.fori_loop(0, qi, full_blocks, 0)

    base = pl.multiple_of(qi * tq, tq)
    for c in range(n_chains):
        step(c, base, (c + 1) * cq, diag_bias)

    o_ref[...] = (acc_sc[...] / l_sc[...]).astype(o_ref.dtype)


def _mla_attn(q, k, v, *, n_heads, tq, tk):
    B, S, _ = q.shape
    assert tq % MLA_CHAIN == 0 and tk % MLA_CHAIN == 0 and tq % tk == 0 and S % tq == 0
    return pl.pallas_call(
        functools.partial(_mla_attn_kernel, tq=tq, tk=tk),
        grid=(B, n_heads, S // tq),
        in_specs=[
            pl.BlockSpec((None, tq, MLA_QK_PAD), lambda b, h, i: (b, i, h)),
            pl.BlockSpec((None, S, MLA_QK_PAD), lambda b, h, i: (b, 0, h)),
            pl.BlockSpec((None, S, V_HEAD_DIM), lambda b, h, i: (b, 0, h)),
        ],
        out_specs=pl.BlockSpec((None, tq, V_HEAD_DIM), lambda b, h, i: (b, i, h)),
        out_shape=jax.ShapeDtypeStruct((B, S, n_heads * V_HEAD_DIM), BF16),
        scratch_shapes=[pltpu.VMEM((tq, LANES), F32)] * 3,
        compiler_params=pltpu.CompilerParams(
            dimension_semantics=("parallel", "parallel", "arbitrary"),
            vmem_limit_bytes=VMEM_LIMIT),
        name="mla_attn",
    )(q, k, v)


def _outproj_kernel(a_ref, b_ref, x_ref, w_ref, g_ref, o_ref, *, a_width):
    y = jnp.dot(a_ref[...], w_ref[0:a_width, :], preferred_element_type=F32)
    y = y + jnp.dot(b_ref[...], w_ref[a_width:, :], preferred_element_type=F32)
    o_ref[...] = x_ref[...] + _rms(y, g_ref[...])


def _outproj(a, b, x2d, w_out, g_post, *, tm):
    T, D = x2d.shape
    const = lambda i: (0, 0)
    row = lambda i: (i, 0)
    return pl.pallas_call(
        functools.partial(_outproj_kernel, a_width=a.shape[1]),
        grid=(T // tm,),
        in_specs=[
            pl.BlockSpec((tm, a.shape[1]), row),
            pl.BlockSpec((tm, b.shape[1]), row),
            pl.BlockSpec((tm, D), row),
            pl.BlockSpec(w_out.shape, const, pipeline_mode=pl.Buffered(1)),
            pl.BlockSpec((1, D), const),
        ],
        out_specs=pl.BlockSpec((tm, D), row),
        out_shape=jax.ShapeDtypeStruct((T, D), F32),
        compiler_params=pltpu.CompilerParams(
            dimension_semantics=("parallel",), vmem_limit_bytes=VMEM_LIMIT),
        name="outproj",
    )(a, b, x2d, w_out, g_post)


def _mlp_kernel(x_ref, gpre_ref, gpost_ref, wu_ref, wd_ref, o_ref, h_sc, acc_sc):
    f = pl.program_id(1)

    @pl.when(f == 0)
    def _():
        h_sc[...] = _rms(x_ref[...], gpre_ref[...]).astype(BF16)

    u = jnp.dot(h_sc[...], wu_ref[...], preferred_element_type=F32)
    u = jnp.square(jnp.maximum(u, 0.0)).astype(BF16)
    part = jnp.dot(u, wd_ref[...], preferred_element_type=F32)

    @pl.when(f == 0)
    def _():
        acc_sc[...] = part

    @pl.when(f > 0)
    def _():
        acc_sc[...] += part

    @pl.when(f == pl.num_programs(1) - 1)
    def _():
        o_ref[...] = x_ref[...] + _rms(acc_sc[...], gpost_ref[...])


def _mlp(x2d, g_pre, g_post, w_up, w_down, *, tm, tf):
    T, D = x2d.shape
    F = w_up.shape[1]
    return pl.pallas_call(
        _mlp_kernel,
        grid=(T // tm, F // tf),
        in_specs=[
            pl.BlockSpec((tm, D), lambda i, f: (i, 0)),
            pl.BlockSpec((1, D), lambda i, f: (0, 0)),
            pl.BlockSpec((1, D), lambda i, f: (0, 0)),
            pl.BlockSpec((D, tf), lambda i, f: (0, f)),
            pl.BlockSpec((tf, D), lambda i, f: (f, 0)),
        ],
        out_specs=pl.BlockSpec((tm, D), lambda i, f: (i, 0)),
        out_shape=jax.ShapeDtypeStruct((T, D), F32),
        scratch_shapes=[pltpu.VMEM((tm, D), BF16), pltpu.VMEM((tm, D), F32)],
        compiler_params=pltpu.CompilerParams(
            dimension_semantics=("parallel", "arbitrary"), vmem_limit_bytes=VMEM_LIMIT),
        name="mlp",
    )(x2d, g_pre, g_post, w_up, w_down)


def _inv_freq_lanes(rot_dim):
    inv = ROPE_THETA ** (-jnp.arange(0, rot_dim, 2, dtype=F32) / rot_dim)
    lanes = jnp.concatenate([inv, inv, jnp.zeros((LANES - rot_dim,), F32)])
    return lanes.reshape(1, LANES)


def _layer(x2d, pos, B, S, norm_attn_pre, norm_attn_post, w_in, q_latent_norm, kv_latent_norm,
           w_uq, w_ukv, w_out, norm_mlp_pre, norm_mlp_post, w_up, w_down):
    D = x2d.shape[1]
    q_rank = q_latent_norm.shape[0]
    kv_rank = kv_latent_norm.shape[0]
    n_mla = w_uq.shape[1] // (QK_NOPE_DIM + QK_ROPE_DIM)
    a_width = (w_in.shape[1] - q_rank - kv_rank - QK_ROPE_DIM) // 3
    n_a = a_width // HEAD_DIM

    w_in_p = jnp.pad(w_in, ((0, 0), (0, LANES - QK_ROPE_DIM))).astype(BF16)
    wq3 = w_uq.reshape(q_rank, n_mla, QK_NOPE_DIM + QK_ROPE_DIM)
    wq_p = jnp.pad(wq3, ((0, 0), (0, 0), (0, MLA_QK_PAD - QK_NOPE_DIM - QK_ROPE_DIM)))
    wq_p = wq_p.reshape(q_rank, n_mla * MLA_QK_PAD).astype(BF16)
    wkv3 = w_ukv.reshape(kv_rank, n_mla, QK_NOPE_DIM + V_HEAD_DIM)
    wk_p = wkv3[:, :, :QK_NOPE_DIM].reshape(kv_rank, n_mla * QK_NOPE_DIM).astype(BF16)
    wv_p = wkv3[:, :, QK_NOPE_DIM:].reshape(kv_rank, n_mla * V_HEAD_DIM).astype(BF16)

    invfa = _inv_freq_lanes(ROT_DIM)
    invfb = _inv_freq_lanes(QK_ROPE_DIM)
    row = lambda g: g.reshape(1, -1)

    aq, ak, av, cq, ckv, kr = _inproj(
        x2d, pos, row(norm_attn_pre), w_in_p, invfa, invfb, row(q_latent_norm),
        row(kv_latent_norm), a_width=a_width, q_rank=q_rank, kv_rank=kv_rank, tm=512)
    q_b, k_b, v_b = _mla_up(cq, ckv, kr, pos, invfb, wq_p, wk_p, wv_p, n_heads=n_mla, tm=512)

    a_out = _dswa(aq.reshape(B, S, a_width), ak.reshape(B, S, a_width),
                  av.reshape(B, S, a_width), n_heads=n_a)
    b_out = _mla_attn(q_b.reshape(B, S, -1), k_b.reshape(B, S, -1), v_b.reshape(B, S, -1),
                      n_heads=n_mla, tq=1024, tk=512)

    x2d = _outproj(a_out.reshape(B * S, -1), b_out.reshape(B * S, -1), x2d,
                   w_out.astype(BF16), row(norm_attn_post), tm=512)
    return _mlp(x2d, row(norm_mlp_pre), row(norm_mlp_post), w_up.astype(BF16),
                w_down.astype(BF16), tm=512, tf=1024)


def kernel(x, positions, norm_attn_pre, norm_attn_post, w_in, q_latent_norm, kv_latent_norm,
           w_uq, w_ukv, w_out, norm_mlp_pre, norm_mlp_post, w_up, w_down):
    B, S, D = x.shape
    x2d = x.reshape(B * S, D)
    pos = positions.astype(F32).reshape(B * S, 1)
    for layer in range(w_in.shape[0]):
        x2d = _layer(x2d, pos, B, S, norm_attn_pre[layer], norm_attn_post[layer], w_in[layer],
                     q_latent_norm[layer], kv_latent_norm[layer], w_uq[layer], w_ukv[layer],
                     w_out[layer], norm_mlp_pre[layer], norm_mlp_post[layer], w_up[layer],
                     w_down[layer])
    return x2d.reshape(B, S, D)
```

```python
import functools

import jax
import jax.numpy as jnp
from jax import lax
from jax.experimental import pallas as pl
from jax.experimental.pallas import tpu as pltpu

F32 = jnp.float32
BF16 = jnp.bfloat16

LANES = 128
HEAD_DIM = 128
ROT_DIM = HEAD_DIM // 4
ROPE_THETA = 500000.0
QK_NOPE_DIM = 128
QK_ROPE_DIM = 64
V_HEAD_DIM = 128
MLA_QK_PAD = 256
Q_BLOCK = 128
DILATIONS = (1, 4, 16)
NORM_EPS = 1e-6
NEG_INF = -1e30
LOG2E = 1.4426950408889634
VMEM_LIMIT = 56 * 1024 * 1024


def _rms(x, gain):
    ms = jnp.mean(x * x, axis=-1, keepdims=True)
    return x * lax.rsqrt(ms + NORM_EPS) * gain


def _rope_tables(pos, inv_freq, half):
    lane = lax.broadcasted_iota(jnp.int32, (1, LANES), 1)
    ang = pos * inv_freq
    sign = jnp.where(lane < half, -1.0, 1.0).astype(F32)
    return jnp.cos(ang), jnp.sin(ang) * sign


def _rope(r, cos, sin, half):
    lane = lax.broadcasted_iota(jnp.int32, (1, LANES), 1)
    partner = jnp.where(lane < half,
                        pltpu.roll(r, LANES - half, 1),
                        pltpu.roll(r, half, 1))
    return r * cos + partner * sin


ROW_SPLIT = 2


def _inproj_kernel(x_ref, pos_ref, g_ref, w_ref, invfa_ref, invfb_ref, gq_ref, gkv_ref,
                   aq_ref, ak_ref, av_ref, cq_ref, ckv_ref, kr_ref, *, a_width, q_rank, kv_rank):
    n_heads = a_width // HEAD_DIM
    q_scale = HEAD_DIM ** -0.5 * LOG2E
    tm = x_ref.shape[0]
    for part in range(ROW_SPLIT):
        rows = slice(part * tm // ROW_SPLIT, (part + 1) * tm // ROW_SPLIT)
        h = _rms(x_ref[rows, :], g_ref[...]).astype(BF16)
        pos = pos_ref[rows, :]
        cos_a, sin_a = _rope_tables(pos, invfa_ref[...], ROT_DIM // 2)
        cos_b, sin_b = _rope_tables(pos, invfb_ref[...], QK_ROPE_DIM // 2)

        r = jnp.dot(h, w_ref[:, 0:a_width], preferred_element_type=F32)
        for hd in range(n_heads):
            sl = slice(hd * HEAD_DIM, (hd + 1) * HEAD_DIM)
            aq_ref[rows, sl] = _rope(r[:, sl], cos_a, sin_a, ROT_DIM // 2) * q_scale
        r = jnp.dot(h, w_ref[:, a_width:2 * a_width], preferred_element_type=F32)
        for hd in range(n_heads):
            sl = slice(hd * HEAD_DIM, (hd + 1) * HEAD_DIM)
            ak_ref[rows, sl] = _rope(r[:, sl], cos_a, sin_a, ROT_DIM // 2)
        av_ref[rows, :] = jnp.dot(h, w_ref[:, 2 * a_width:3 * a_width],
                                  preferred_element_type=F32)

        r = jnp.dot(h, w_ref[:, 3 * a_width:], preferred_element_type=F32)
        cq_ref[rows, :] = _rms(r[:, 0:q_rank], gq_ref[...]).astype(BF16)
        ckv_ref[rows, :] = _rms(r[:, q_rank:q_rank + kv_rank], gkv_ref[...]).astype(BF16)
        kr = r[:, q_rank + kv_rank:q_rank + kv_rank + LANES]
        kr_ref[rows, :] = _rope(kr, cos_b, sin_b, QK_ROPE_DIM // 2).astype(BF16)


def _inproj(x2d, pos, g_pre, w_in, invfa, invfb, gq, gkv, *, a_width, q_rank, kv_rank, tm):
    T, D = x2d.shape
    ncols = w_in.shape[1]
    const = lambda i: (0, 0)
    row = lambda i: (i, 0)
    kern = functools.partial(_inproj_kernel, a_width=a_width, q_rank=q_rank, kv_rank=kv_rank)
    return pl.pallas_call(
        kern,
        grid=(T // tm,),
        in_specs=[
            pl.BlockSpec((tm, D), row),
            pl.BlockSpec((tm, 1), row),
            pl.BlockSpec((1, D), const),
            pl.BlockSpec((D, ncols), const, pipeline_mode=pl.Buffered(1)),
            pl.BlockSpec((1, LANES), const),
            pl.BlockSpec((1, LANES), const),
            pl.BlockSpec((1, q_rank), const),
            pl.BlockSpec((1, kv_rank), const),
        ],
        out_specs=[
            pl.BlockSpec((tm, a_width), row),
            pl.BlockSpec((tm, a_width), row),
            pl.BlockSpec((tm, a_width), row),
            pl.BlockSpec((tm, q_rank), row),
            pl.BlockSpec((tm, kv_rank), row),
            pl.BlockSpec((tm, LANES), row),
        ],
        out_shape=[
            jax.ShapeDtypeStruct((T, a_width), F32),
            jax.ShapeDtypeStruct((T, a_width), F32),
            jax.ShapeDtypeStruct((T, a_width), F32),
            jax.ShapeDtypeStruct((T, q_rank), BF16),
            jax.ShapeDtypeStruct((T, kv_rank), BF16),
            jax.ShapeDtypeStruct((T, LANES), BF16),
        ],
        compiler_params=pltpu.CompilerParams(
            dimension_semantics=("parallel",), vmem_limit_bytes=VMEM_LIMIT),
        name="inproj",
    )(x2d, pos, g_pre, w_in, invfa, invfb, gq, gkv)


def _mla_up_kernel(cq_ref, ckv_ref, kr_ref, pos_ref, invfb_ref, wq_ref, wk_ref, wv_ref,
                   q_ref, k_ref, v_ref, *, n_heads):
    scale = (QK_NOPE_DIM + QK_ROPE_DIM) ** -0.5 * LOG2E
    tm = cq_ref.shape[0]
    for part in range(ROW_SPLIT):
        rows = slice(part * tm // ROW_SPLIT, (part + 1) * tm // ROW_SPLIT)
        cos_b, sin_b = _rope_tables(pos_ref[rows, :], invfb_ref[...], QK_ROPE_DIM // 2)
        rq = jnp.dot(cq_ref[rows, :], wq_ref[...], preferred_element_type=F32)
        ckv = ckv_ref[rows, :]
        rk = jnp.dot(ckv, wk_ref[...], preferred_element_type=F32)
        kr = kr_ref[rows, :]
        for hd in range(n_heads):
            base = hd * MLA_QK_PAD
            q_ref[rows, base:base + LANES] = (rq[:, base:base + LANES] * scale).astype(BF16)
            q_rot = _rope(rq[:, base + LANES:base + 2 * LANES], cos_b, sin_b, QK_ROPE_DIM // 2)
            q_ref[rows, base + LANES:base + 2 * LANES] = (q_rot * scale).astype(BF16)
            k_ref[rows, base:base + LANES] = rk[:, hd * LANES:(hd + 1) * LANES].astype(BF16)
            k_ref[rows, base + LANES:base + 2 * LANES] = kr
        v_ref[rows, :] = jnp.dot(ckv, wv_ref[...], preferred_element_type=F32).astype(BF16)


def _mla_up(cq, ckv, kr, pos, invfb, wq, wk, wv, *, n_heads, tm):
    T = cq.shape[0]
    const = lambda i: (0, 0)
    row = lambda i: (i, 0)
    qk_w = n_heads * MLA_QK_PAD
    v_w = n_heads * V_HEAD_DIM
    return pl.pallas_call(
        functools.partial(_mla_up_kernel, n_heads=n_heads),
        grid=(T // tm,),
        in_specs=[
            pl.BlockSpec((tm, cq.shape[1]), row),
            pl.BlockSpec((tm, ckv.shape[1]), row),
            pl.BlockSpec((tm, LANES), row),
            pl.BlockSpec((tm, 1), row),
            pl.BlockSpec((1, LANES), const),
            pl.BlockSpec(wq.shape, const),
            pl.BlockSpec(wk.shape, const),
            pl.BlockSpec(wv.shape, const),
        ],
        out_specs=[
            pl.BlockSpec((tm, qk_w), row),
            pl.BlockSpec((tm, qk_w), row),
            pl.BlockSpec((tm, v_w), row),
        ],
        out_shape=[
            jax.ShapeDtypeStruct((T, qk_w), BF16),
            jax.ShapeDtypeStruct((T, qk_w), BF16),
            jax.ShapeDtypeStruct((T, v_w), BF16),
        ],
        compiler_params=pltpu.CompilerParams(
            dimension_semantics=("parallel",), vmem_limit_bytes=VMEM_LIMIT),
        name="mla_up",
    )(cq, ckv, kr, pos, invfb, wq, wk, wv)


DSWA_GROUP = 16
MAX_DIL = 16
NT_DIMS = (((1,), (1,)), ((), ()))


def _aligned(x, multiple):
    return x if isinstance(x, int) else pl.multiple_of(x, multiple)


def _softmax_block(q, k, v, bias, prev):
    nk = k.shape[0]
    s = lax.dot_general(q, k, NT_DIMS, preferred_element_type=F32)
    blocks = [s[:, i * LANES:(i + 1) * LANES] for i in range(nk // LANES)]
    if bias is not None:
        nb = bias.shape[1] // LANES
        blocks[-nb:] = [b + bias[:, i * LANES:(i + 1) * LANES]
                        for i, b in enumerate(blocks[-nb:])]
    m_cur = jnp.max(functools.reduce(jnp.maximum, blocks), axis=1, keepdims=True)
    if prev is None:
        m_new = jnp.broadcast_to(m_cur, (q.shape[0], LANES))
    else:
        m_prev, l_prev, acc_prev = prev
        m_new = jnp.maximum(m_prev, m_cur)
        alpha = jnp.exp2(m_prev - m_new)
    p = jnp.concatenate([jnp.exp2(b - m_new) for b in blocks], axis=1).astype(BF16)
    v_aug = jnp.concatenate([v, jnp.ones((nk, LANES), BF16)], axis=1)
    pv = jnp.dot(p, v_aug, preferred_element_type=F32)
    acc_cur, l_cur = pv[:, :LANES], pv[:, LANES:]
    if prev is None:
        return m_new, l_cur, acc_cur
    return m_new, alpha * l_prev + l_cur, alpha * acc_prev + acc_cur


def _dswa_kernel(q_ref, k_ref, v_ref, o_ref, xq, xk, xv, mx, lx, ax, mn, ln, an, x4, *, seq):
    U = seq // MAX_DIL
    QB = Q_BLOCK
    NEG = NEG_INF

    def band_bias(dist):
        return jnp.where(dist >= 0, jnp.where(dist <= QB, 0.0, NEG), NEG).astype(F32)

    i1 = lax.broadcasted_iota(jnp.int32, (QB, QB), 0)
    j1 = lax.broadcasted_iota(jnp.int32, (QB, QB), 1)
    i2 = lax.broadcasted_iota(jnp.int32, (QB, 2 * QB), 0)
    j2 = lax.broadcasted_iota(jnp.int32, (QB, 2 * QB), 1)
    bias_first = band_bias(i1 - j1)
    bias_band = band_bias(i2 + QB - j2)
    qc = QB // 4
    lo = lambda x, n: x & (n - 1)
    hi = lambda x, n: x >> (n.bit_length() - 1)
    d4_first = band_bias(4 * (lo(i1, qc) - lo(j1, qc)) + hi(i1, qc) - hi(j1, qc))
    d4_band = band_bias(4 * (lo(i2, qc) - lo(j2, 2 * qc) + qc) + hi(i2, qc) - hi(j2, 2 * qc))

    for src, dst in ((q_ref, xq), (k_ref, xk), (v_ref, xv)):
        for r4 in range(4):
            x4[r4] = src[pl.ds(r4, 4 * U, stride=4), :]
        for r in range(MAX_DIL):
            dst[r] = x4[r % 4, pl.ds(r // 4, U, stride=4), :].astype(BF16)

    def d16_group(g, carry):
        out = []
        for j in range(DSWA_GROUP):
            r = g * DSWA_GROUP + j
            out.append((r, 0, _softmax_block(xq[r, 0:QB, :], xk[r, 0:QB, :], xv[r, 0:QB, :],
                                             bias_first, None)))
            for n in range(1, U // QB):
                ks = slice((n - 1) * QB, (n + 1) * QB)
                out.append((r, n, _softmax_block(xq[r, n * QB:(n + 1) * QB, :], xk[r, ks, :],
                                                 xv[r, ks, :], bias_band, None)))
        for r, n, (m_new, l_new, acc_new) in out:
            qs = slice(n * QB, (n + 1) * QB)
            mx[r, qs, :] = m_new
            lx[r, qs, :] = l_new
            ax[r, qs, :] = acc_new
        return carry
    lax.fori_loop(0, MAX_DIL // DSWA_GROUP, d16_group, 0)

    def d4_tile(r4, a, first):
        def gather(ref, start, size):
            return jnp.concatenate([ref[4 * qq + r4, pl.ds(start, size), :] for qq in range(4)],
                                   axis=0)
        q = gather(xq, a, qc)
        prev = (gather(mx, a, qc), gather(lx, a, qc), gather(ax, a, qc))
        if first:
            k, v, bias = gather(xk, a, qc), gather(xv, a, qc), d4_first
        else:
            k, v, bias = gather(xk, a - qc, 2 * qc), gather(xv, a - qc, 2 * qc), d4_band
        return _softmax_block(q, k, v, bias, prev)

    def d4_store(r4, a, res):
        for ref, val in zip((mx, lx, ax), res):
            for qq in range(4):
                ref[4 * qq + r4, pl.ds(a, qc), :] = val[qq * qc:(qq + 1) * qc]

    def d4_group(g, first_group):
        offs = [_aligned((g * (DSWA_GROUP // 4) + i) * qc, qc) for i in range(DSWA_GROUP // 4)]
        res = [(r4, a, d4_tile(r4, a, first_group and i == 0))
               for i, a in enumerate(offs) for r4 in range(4)]
        for r4, a, val in res:
            d4_store(r4, a, val)

    d4_group(0, True)
    lax.fori_loop(1, U // qc // (DSWA_GROUP // 4), lambda g, c: (d4_group(g, False), c)[1], 0)

    for src, dst in ((mx, mn), (lx, ln), (ax, an)):
        for r in range(MAX_DIL):
            x4[r % 4, pl.ds(r // 4, U, stride=4), :] = src[r]
        for r4 in range(4):
            dst[pl.ds(r4, 4 * U, stride=4), :] = x4[r4]

    def d1_tile(n, first):
        qs = pl.ds(_aligned(n * QB, QB), QB)
        if first:
            ks, bias = qs, bias_first
        else:
            ks, bias = pl.ds(_aligned((n - 1) * QB, QB), 2 * QB), bias_band
        prev = (mn[qs, :], ln[qs, :], an[qs, :])
        _, l_new, acc_new = _softmax_block(q_ref[qs, :].astype(BF16), k_ref[ks, :].astype(BF16),
                                           v_ref[ks, :].astype(BF16), bias, prev)
        return qs, (acc_new / l_new).astype(o_ref.dtype)

    def d1_group(g, first_group):
        res = [d1_tile(g * DSWA_GROUP + j, first_group and j == 0) for j in range(DSWA_GROUP)]
        for qs, val in res:
            o_ref[qs, :] = val

    d1_group(0, True)
    lax.fori_loop(1, seq // QB // DSWA_GROUP, lambda g, c: (d1_group(g, False), c)[1], 0)


def _dswa(aq, ak, av, *, n_heads):
    B, S, _ = aq.shape
    U = S // MAX_DIL
    assert DILATIONS == (1, 4, MAX_DIL) and MAX_DIL % DSWA_GROUP == 0
    assert U % Q_BLOCK == 0 and (S // Q_BLOCK) % DSWA_GROUP == 0
    spec = pl.BlockSpec((None, S, HEAD_DIM), lambda b, h: (b, 0, h))
    x16 = lambda dt: pltpu.VMEM((MAX_DIL, U, LANES), dt)
    return pl.pallas_call(
        functools.partial(_dswa_kernel, seq=S),
        grid=(B, n_heads),
        in_specs=[spec, spec, spec],
        out_specs=spec,
        out_shape=jax.ShapeDtypeStruct(aq.shape, BF16),
        scratch_shapes=([x16(BF16)] * 3 + [x16(F32)] * 3 + [pltpu.VMEM((S, LANES), F32)] * 3
                        + [pltpu.VMEM((4, S // 4, LANES), F32)]),
        compiler_params=pltpu.CompilerParams(
            dimension_semantics=("parallel", "parallel"), vmem_limit_bytes=VMEM_LIMIT),
        name="dswa",
    )(aq, ak, av)


MLA_CHAIN = 256


def _mla_attn_kernel(q_ref, k_ref, v_ref, o_ref, *, seq, tk):
    cq = MLA_CHAIN
    n_chains = seq // cq
    ri = lax.broadcasted_iota(jnp.int32, (cq, cq), 0)
    ci = lax.broadcasted_iota(jnp.int32, (cq, cq), 1)
    diag_bias = jnp.where(ci <= ri, 0.0, NEG_INF).astype(F32)

    plans = []
    for c in range(n_chains):
        n_full = (c * cq) // tk
        plan = [(j * tk, tk, None) for j in range(n_full)]
        plan.append((n_full * tk, (c + 1) * cq - n_full * tk, diag_bias))
        plans.append(plan)

    state = [None] * n_chains
    for t in range(max(len(p) for p in plans)):
        for c in reversed(range(n_chains)):
            if t >= len(plans[c]):
                continue
            start, nk, bias = plans[c][t]
            rows = slice(c * cq, (c + 1) * cq)
            ks = slice(start, start + nk)
            state[c] = _softmax_block(q_ref[rows, :], k_ref[ks, :], v_ref[ks, :], bias, state[c])
            if t == len(plans[c]) - 1:
                _, l, acc = state[c]
                o_ref[rows, :] = (acc / l).astype(o_ref.dtype)


def _mla_attn(q, k, v, *, n_heads, tk):
    B, S, _ = q.shape
    assert tk % MLA_CHAIN == 0 and S % tk == 0
    head = lambda width: pl.BlockSpec((None, S, width), lambda b, h: (b, 0, h))
    return pl.pallas_call(
        functools.partial(_mla_attn_kernel, seq=S, tk=tk),
        grid=(B, n_heads),
        in_specs=[head(MLA_QK_PAD), head(MLA_QK_PAD), head(V_HEAD_DIM)],
        out_specs=head(V_HEAD_DIM),
        out_shape=jax.ShapeDtypeStruct((B, S, n_heads * V_HEAD_DIM), BF16),
        compiler_params=pltpu.CompilerParams(
            dimension_semantics=("parallel", "parallel"), vmem_limit_bytes=VMEM_LIMIT),
        name="mla_attn",
    )(q, k, v)


def _outproj_kernel(a_ref, b_ref, x_ref, w_ref, g_ref, o_ref, *, a_width):
    tm = x_ref.shape[0]
    for part in range(ROW_SPLIT):
        rows = slice(part * tm // ROW_SPLIT, (part + 1) * tm // ROW_SPLIT)
        y = jnp.dot(a_ref[rows, :], w_ref[0:a_width, :], preferred_element_type=F32)
        y = y + jnp.dot(b_ref[rows, :], w_ref[a_width:, :], preferred_element_type=F32)
        o_ref[rows, :] = x_ref[rows, :] + _rms(y, g_ref[...])


def _outproj(a, b, x2d, w_out, g_post, *, tm):
    T, D = x2d.shape
    const = lambda i: (0, 0)
    row = lambda i: (i, 0)
    return pl.pallas_call(
        functools.partial(_outproj_kernel, a_width=a.shape[1]),
        grid=(T // tm,),
        in_specs=[
            pl.BlockSpec((tm, a.shape[1]), row),
            pl.BlockSpec((tm, b.shape[1]), row),
            pl.BlockSpec((tm, D), row),
            pl.BlockSpec(w_out.shape, const, pipeline_mode=pl.Buffered(1)),
            pl.BlockSpec((1, D), const),
        ],
        out_specs=pl.BlockSpec((tm, D), row),
        out_shape=jax.ShapeDtypeStruct((T, D), F32),
        compiler_params=pltpu.CompilerParams(
            dimension_semantics=("parallel",), vmem_limit_bytes=VMEM_LIMIT),
        name="outproj",
    )(a, b, x2d, w_out, g_post)


def _mlp_kernel(x_ref, gpre_ref, gpost_ref, wu_ref, wd_ref, o_ref, h_sc, acc_sc):
    f = pl.program_id(1)
    last = pl.num_programs(1) - 1
    tm = x_ref.shape[0]

    def step(first, final):
        for part in range(ROW_SPLIT):
            rows = slice(part * tm // ROW_SPLIT, (part + 1) * tm // ROW_SPLIT)
            if first:
                h = _rms(x_ref[rows, :], gpre_ref[...]).astype(BF16)
                h_sc[rows, :] = h
            else:
                h = h_sc[rows, :]
            u = jnp.dot(h, wu_ref[...], preferred_element_type=F32)
            u = jnp.square(jnp.maximum(u, 0.0)).astype(BF16)
            y = jnp.dot(u, wd_ref[...], preferred_element_type=F32)
            acc = y if first else acc_sc[rows, :] + y
            if final:
                o_ref[rows, :] = x_ref[rows, :] + _rms(acc, gpost_ref[...])
            else:
                acc_sc[rows, :] = acc

    pl.when(f == 0)(lambda: step(True, False))
    pl.when(jnp.logical_and(f > 0, f < last))(lambda: step(False, False))
    pl.when(f == last)(lambda: step(False, True))


def _mlp(x2d, g_pre, g_post, w_up, w_down, *, tm, tf):
    T, D = x2d.shape
    F = w_up.shape[1]
    return pl.pallas_call(
        _mlp_kernel,
        grid=(T // tm, F // tf),
        in_specs=[
            pl.BlockSpec((tm, D), lambda i, f: (i, 0)),
            pl.BlockSpec((1, D), lambda i, f: (0, 0)),
            pl.BlockSpec((1, D), lambda i, f: (0, 0)),
            pl.BlockSpec((D, tf), lambda i, f: (0, f)),
            pl.BlockSpec((tf, D), lambda i, f: (f, 0)),
        ],
        out_specs=pl.BlockSpec((tm, D), lambda i, f: (i, 0)),
        out_shape=jax.ShapeDtypeStruct((T, D), F32),
        scratch_shapes=[pltpu.VMEM((tm, D), BF16), pltpu.VMEM((tm, D), F32)],
        compiler_params=pltpu.CompilerParams(
            dimension_semantics=("parallel", "arbitrary"), vmem_limit_bytes=VMEM_LIMIT),
        name="mlp",
    )(x2d, g_pre, g_post, w_up, w_down)


def _inv_freq_lanes(rot_dim):
    inv = ROPE_THETA ** (-jnp.arange(0, rot_dim, 2, dtype=F32) / rot_dim)
    lanes = jnp.concatenate([inv, inv, jnp.zeros((LANES - rot_dim,), F32)])
    return lanes.reshape(1, LANES)


def _layer(x2d, pos, B, S, norm_attn_pre, norm_attn_post, w_in, q_latent_norm, kv_latent_norm,
           w_uq, w_ukv, w_out, norm_mlp_pre, norm_mlp_post, w_up, w_down):
    D = x2d.shape[1]
    q_rank = q_latent_norm.shape[0]
    kv_rank = kv_latent_norm.shape[0]
    n_mla = w_uq.shape[1] // (QK_NOPE_DIM + QK_ROPE_DIM)
    a_width = (w_in.shape[1] - q_rank - kv_rank - QK_ROPE_DIM) // 3
    n_a = a_width // HEAD_DIM

    w_in_p = jnp.pad(w_in, ((0, 0), (0, LANES - QK_ROPE_DIM))).astype(BF16)
    wq3 = w_uq.reshape(q_rank, n_mla, QK_NOPE_DIM + QK_ROPE_DIM)
    wq_p = jnp.pad(wq3, ((0, 0), (0, 0), (0, MLA_QK_PAD - QK_NOPE_DIM - QK_ROPE_DIM)))
    wq_p = wq_p.reshape(q_rank, n_mla * MLA_QK_PAD).astype(BF16)
    wkv3 = w_ukv.reshape(kv_rank, n_mla, QK_NOPE_DIM + V_HEAD_DIM)
    wk_p = wkv3[:, :, :QK_NOPE_DIM].reshape(kv_rank, n_mla * QK_NOPE_DIM).astype(BF16)
    wv_p = wkv3[:, :, QK_NOPE_DIM:].reshape(kv_rank, n_mla * V_HEAD_DIM).astype(BF16)

    invfa = _inv_freq_lanes(ROT_DIM)
    invfb = _inv_freq_lanes(QK_ROPE_DIM)
    row = lambda g: g.reshape(1, -1)

    aq, ak, av, cq, ckv, kr = _inproj(
        x2d, pos, row(norm_attn_pre), w_in_p, invfa, invfb, row(q_latent_norm),
        row(kv_latent_norm), a_width=a_width, q_rank=q_rank, kv_rank=kv_rank, tm=512)
    q_b, k_b, v_b = _mla_up(cq, ckv, kr, pos, invfb, wq_p, wk_p, wv_p, n_heads=n_mla, tm=512)

    a_out = _dswa(aq.reshape(B, S, a_width), ak.reshape(B, S, a_width),
                  av.reshape(B, S, a_width), n_heads=n_a)
    b_out = _mla_attn(q_b.reshape(B, S, -1), k_b.reshape(B, S, -1), v_b.reshape(B, S, -1),
                      n_heads=n_mla, tk=512)

    x2d = _outproj(a_out.reshape(B * S, -1), b_out.reshape(B * S, -1), x2d,
                   w_out.astype(BF16), row(norm_attn_post), tm=512)
    return _mlp(x2d, row(norm_mlp_pre), row(norm_mlp_post), w_up.astype(BF16),
                w_down.astype(BF16), tm=512, tf=1024)


def kernel(x, positions, norm_attn_pre, norm_attn_post, w_in, q_latent_norm, kv_latent_norm,
           w_uq, w_ukv, w_out, norm_mlp_pre, norm_mlp_post, w_up, w_down):
    B, S, D = x.shape
    x2d = x.reshape(B * S, D)
    pos = positions.astype(F32).reshape(B * S, 1)
    for layer in range(w_in.shape[0]):
        x2d = _layer(x2d, pos, B, S, norm_attn_pre[layer], norm_attn_post[layer], w_in[layer],
                     q_latent_norm[layer], kv_latent_norm[layer], w_uq[layer], w_ukv[layer],
                     w_out[layer], norm_mlp_pre[layer], norm_mlp_post[layer], w_up[layer],
                     w_down[layer])
    return x2d.reshape(B, S, D)
```

```python
import functools

import jax
import jax.numpy as jnp
from jax import lax
from jax.experimental import pallas as pl
from jax.experimental.pallas import tpu as pltpu

F32 = jnp.float32
BF16 = jnp.bfloat16

LANES = 128
HEAD_DIM = 128
ROT_DIM = HEAD_DIM // 4
ROPE_THETA = 500000.0
QK_NOPE_DIM = 128
QK_ROPE_DIM = 64
V_HEAD_DIM = 128
MLA_QK_PAD = 256
Q_BLOCK = 128
DILATIONS = (1, 4, 16)
NORM_EPS = 1e-6
NEG_INF = -1e30
LOG2E = 1.4426950408889634
VMEM_LIMIT = 56 * 1024 * 1024


def _rms(x, gain):
    ms = jnp.mean(x * x, axis=-1, keepdims=True)
    return x * lax.rsqrt(ms + NORM_EPS) * gain


def _rope_tables(pos, inv_freq, half):
    lane = lax.broadcasted_iota(jnp.int32, (1, LANES), 1)
    ang = pos * inv_freq
    sign = jnp.where(lane < half, -1.0, 1.0).astype(F32)
    return jnp.cos(ang), jnp.sin(ang) * sign


def _rope(r, cos, sin, half):
    lane = lax.broadcasted_iota(jnp.int32, (1, LANES), 1)
    partner = jnp.where(lane < half,
                        pltpu.roll(r, LANES - half, 1),
                        pltpu.roll(r, half, 1))
    return r * cos + partner * sin


ROW_SPLIT = 2


def _inproj_kernel(x_ref, pos_ref, g_ref, w_ref, invfa_ref, invfb_ref, gq_ref, gkv_ref,
                   aq_ref, ak_ref, av_ref, cq_ref, ckv_ref, kr_ref, *, a_width, q_rank, kv_rank):
    n_heads = a_width // HEAD_DIM
    q_scale = HEAD_DIM ** -0.5 * LOG2E
    tm = x_ref.shape[0]
    for part in range(ROW_SPLIT):
        rows = slice(part * tm // ROW_SPLIT, (part + 1) * tm // ROW_SPLIT)
        h = _rms(x_ref[rows, :], g_ref[...]).astype(BF16)
        pos = pos_ref[rows, :]
        cos_a, sin_a = _rope_tables(pos, invfa_ref[...], ROT_DIM // 2)
        cos_b, sin_b = _rope_tables(pos, invfb_ref[...], QK_ROPE_DIM // 2)

        r = jnp.dot(h, w_ref[:, 0:a_width], preferred_element_type=F32)
        for hd in range(n_heads):
            sl = slice(hd * HEAD_DIM, (hd + 1) * HEAD_DIM)
            aq_ref[rows, sl] = _rope(r[:, sl], cos_a, sin_a, ROT_DIM // 2) * q_scale
        r = jnp.dot(h, w_ref[:, a_width:2 * a_width], preferred_element_type=F32)
        for hd in range(n_heads):
            sl = slice(hd * HEAD_DIM, (hd + 1) * HEAD_DIM)
            ak_ref[rows, sl] = _rope(r[:, sl], cos_a, sin_a, ROT_DIM // 2)
        av_ref[rows, :] = jnp.dot(h, w_ref[:, 2 * a_width:3 * a_width],
                                  preferred_element_type=F32)

        r = jnp.dot(h, w_ref[:, 3 * a_width:], preferred_element_type=F32)
        cq_ref[rows, :] = _rms(r[:, 0:q_rank], gq_ref[...]).astype(BF16)
        ckv_ref[rows, :] = _rms(r[:, q_rank:q_rank + kv_rank], gkv_ref[...]).astype(BF16)
        kr = r[:, q_rank + kv_rank:q_rank + kv_rank + LANES]
        kr_ref[rows, :] = _rope(kr, cos_b, sin_b, QK_ROPE_DIM // 2).astype(BF16)


def _inproj(x2d, pos, g_pre, w_in, invfa, invfb, gq, gkv, *, a_width, q_rank, kv_rank, tm):
    T, D = x2d.shape
    ncols = w_in.shape[1]
    const = lambda i: (0, 0)
    row = lambda i: (i, 0)
    kern = functools.partial(_inproj_kernel, a_width=a_width, q_rank=q_rank, kv_rank=kv_rank)
    return pl.pallas_call(
        kern,
        grid=(T // tm,),
        in_specs=[
            pl.BlockSpec((tm, D), row),
            pl.BlockSpec((tm, 1), row),
            pl.BlockSpec((1, D), const),
            pl.BlockSpec((D, ncols), const, pipeline_mode=pl.Buffered(1)),
            pl.BlockSpec((1, LANES), const),
            pl.BlockSpec((1, LANES), const),
            pl.BlockSpec((1, q_rank), const),
            pl.BlockSpec((1, kv_rank), const),
        ],
        out_specs=[
            pl.BlockSpec((tm, a_width), row),
            pl.BlockSpec((tm, a_width), row),
            pl.BlockSpec((tm, a_width), row),
            pl.BlockSpec((tm, q_rank), row),
            pl.BlockSpec((tm, kv_rank), row),
            pl.BlockSpec((tm, LANES), row),
        ],
        out_shape=[
            jax.ShapeDtypeStruct((T, a_width), F32),
            jax.ShapeDtypeStruct((T, a_width), F32),
            jax.ShapeDtypeStruct((T, a_width), F32),
            jax.ShapeDtypeStruct((T, q_rank), BF16),
            jax.ShapeDtypeStruct((T, kv_rank), BF16),
            jax.ShapeDtypeStruct((T, LANES), BF16),
        ],
        compiler_params=pltpu.CompilerParams(
            dimension_semantics=("parallel",), vmem_limit_bytes=VMEM_LIMIT),
        name="inproj",
    )(x2d, pos, g_pre, w_in, invfa, invfb, gq, gkv)


def _mla_up_kernel(cq_ref, ckv_ref, kr_ref, pos_ref, invfb_ref, wq_ref, wk_ref, wv_ref,
                   q_ref, k_ref, v_ref, *, n_heads):
    scale = (QK_NOPE_DIM + QK_ROPE_DIM) ** -0.5 * LOG2E
    tm = cq_ref.shape[0]
    for part in range(ROW_SPLIT):
        rows = slice(part * tm // ROW_SPLIT, (part + 1) * tm // ROW_SPLIT)
        cos_b, sin_b = _rope_tables(pos_ref[rows, :], invfb_ref[...], QK_ROPE_DIM // 2)
        rq = jnp.dot(cq_ref[rows, :], wq_ref[...], preferred_element_type=F32)
        ckv = ckv_ref[rows, :]
        rk = jnp.dot(ckv, wk_ref[...], preferred_element_type=F32)
        kr = kr_ref[rows, :]
        for hd in range(n_heads):
            base = hd * MLA_QK_PAD
            q_ref[rows, base:base + LANES] = (rq[:, base:base + LANES] * scale).astype(BF16)
            q_rot = _rope(rq[:, base + LANES:base + 2 * LANES], cos_b, sin_b, QK_ROPE_DIM // 2)
            q_ref[rows, base + LANES:base + 2 * LANES] = (q_rot * scale).astype(BF16)
            k_ref[rows, base:base + LANES] = rk[:, hd * LANES:(hd + 1) * LANES].astype(BF16)
            k_ref[rows, base + LANES:base + 2 * LANES] = kr
        v_ref[rows, :] = jnp.dot(ckv, wv_ref[...], preferred_element_type=F32).astype(BF16)


def _mla_up(cq, ckv, kr, pos, invfb, wq, wk, wv, *, n_heads, tm):
    T = cq.shape[0]
    const = lambda i: (0, 0)
    row = lambda i: (i, 0)
    qk_w = n_heads * MLA_QK_PAD
    v_w = n_heads * V_HEAD_DIM
    return pl.pallas_call(
        functools.partial(_mla_up_kernel, n_heads=n_heads),
        grid=(T // tm,),
        in_specs=[
            pl.BlockSpec((tm, cq.shape[1]), row),
            pl.BlockSpec((tm, ckv.shape[1]), row),
            pl.BlockSpec((tm, LANES), row),
            pl.BlockSpec((tm, 1), row),
            pl.BlockSpec((1, LANES), const),
            pl.BlockSpec(wq.shape, const),
            pl.BlockSpec(wk.shape, const),
            pl.BlockSpec(wv.shape, const),
        ],
        out_specs=[
            pl.BlockSpec((tm, qk_w), row),
            pl.BlockSpec((tm, qk_w), row),
            pl.BlockSpec((tm, v_w), row),
        ],
        out_shape=[
            jax.ShapeDtypeStruct((T, qk_w), BF16),
            jax.ShapeDtypeStruct((T, qk_w), BF16),
            jax.ShapeDtypeStruct((T, v_w), BF16),
        ],
        compiler_params=pltpu.CompilerParams(
            dimension_semantics=("parallel",), vmem_limit_bytes=VMEM_LIMIT),
        name="mla_up",
    )(cq, ckv, kr, pos, invfb, wq, wk, wv)


DSWA_GROUP = 16
MAX_DIL = 16
NT_DIMS = (((1,), (1,)), ((), ()))


def _aligned(x, multiple):
    return x if isinstance(x, int) else pl.multiple_of(x, multiple)


def _softmax_block(q, k, v, bias, prev):
    nk = k.shape[0]
    s = lax.dot_general(q, k, NT_DIMS, preferred_element_type=F32)
    blocks = [s[:, i * LANES:(i + 1) * LANES] for i in range(nk // LANES)]
    if bias is not None:
        nb = bias.shape[1] // LANES
        blocks[-nb:] = [b + bias[:, i * LANES:(i + 1) * LANES]
                        for i, b in enumerate(blocks[-nb:])]
    m_cur = jnp.max(functools.reduce(jnp.maximum, blocks), axis=1, keepdims=True)
    if prev is None:
        m_new = jnp.broadcast_to(m_cur, (q.shape[0], LANES))
    else:
        m_prev, l_prev, acc_prev = prev
        m_new = jnp.maximum(m_prev, m_cur)
        alpha = jnp.exp2(m_prev - m_new)
    p = jnp.concatenate([jnp.exp2(b - m_new) for b in blocks], axis=1).astype(BF16)
    v_aug = jnp.concatenate([v, jnp.ones((nk, LANES), BF16)], axis=1)
    pv = jnp.dot(p, v_aug, preferred_element_type=F32)
    acc_cur, l_cur = pv[:, :LANES], pv[:, LANES:]
    if prev is None:
        return m_new, l_cur, acc_cur
    return m_new, alpha * l_prev + l_cur, alpha * acc_prev + acc_cur


def _dswa_kernel(q_ref, k_ref, v_ref, o_ref, xq, xk, xv, mx, lx, ax, mn, ln, an, x4, *, seq):
    U = seq // MAX_DIL
    QB = Q_BLOCK
    NEG = NEG_INF

    def band_bias(dist):
        return jnp.where(dist >= 0, jnp.where(dist <= QB, 0.0, NEG), NEG).astype(F32)

    i1 = lax.broadcasted_iota(jnp.int32, (QB, QB), 0)
    j1 = lax.broadcasted_iota(jnp.int32, (QB, QB), 1)
    i2 = lax.broadcasted_iota(jnp.int32, (QB, 2 * QB), 0)
    j2 = lax.broadcasted_iota(jnp.int32, (QB, 2 * QB), 1)
    bias_first = band_bias(i1 - j1)
    bias_band = band_bias(i2 + QB - j2)
    qc = QB // 4
    lo = lambda x, n: x & (n - 1)
    hi = lambda x, n: x >> (n.bit_length() - 1)
    d4_first = band_bias(4 * (lo(i1, qc) - lo(j1, qc)) + hi(i1, qc) - hi(j1, qc))
    d4_band = band_bias(4 * (lo(i2, qc) - lo(j2, 2 * qc) + qc) + hi(i2, qc) - hi(j2, 2 * qc))

    for src, dst in ((q_ref, xq), (k_ref, xk), (v_ref, xv)):
        for r4 in range(4):
            x4[r4] = src[pl.ds(r4, 4 * U, stride=4), :]
        for r in range(MAX_DIL):
            dst[r] = x4[r % 4, pl.ds(r // 4, U, stride=4), :].astype(BF16)

    def d16_group(g, carry):
        out = []
        for j in range(DSWA_GROUP):
            r = g * DSWA_GROUP + j
            out.append((r, 0, _softmax_block(xq[r, 0:QB, :], xk[r, 0:QB, :], xv[r, 0:QB, :],
                                             bias_first, None)))
            for n in range(1, U // QB):
                ks = slice((n - 1) * QB, (n + 1) * QB)
                out.append((r, n, _softmax_block(xq[r, n * QB:(n + 1) * QB, :], xk[r, ks, :],
                                                 xv[r, ks, :], bias_band, None)))
        for r, n, (m_new, l_new, acc_new) in out:
            qs = slice(n * QB, (n + 1) * QB)
            mx[r, qs, :] = m_new
            lx[r, qs, :] = l_new
            ax[r, qs, :] = acc_new
        return carry
    lax.fori_loop(0, MAX_DIL // DSWA_GROUP, d16_group, 0)

    def d4_tile(r4, a, first):
        def gather(ref, start, size):
            return jnp.concatenate([ref[4 * qq + r4, pl.ds(start, size), :] for qq in range(4)],
                                   axis=0)
        q = gather(xq, a, qc)
        prev = (gather(mx, a, qc), gather(lx, a, qc), gather(ax, a, qc))
        if first:
            k, v, bias = gather(xk, a, qc), gather(xv, a, qc), d4_first
        else:
            k, v, bias = gather(xk, a - qc, 2 * qc), gather(xv, a - qc, 2 * qc), d4_band
        return _softmax_block(q, k, v, bias, prev)

    def d4_store(r4, a, res):
        for ref, val in zip((mx, lx, ax), res):
            for qq in range(4):
                ref[4 * qq + r4, pl.ds(a, qc), :] = val[qq * qc:(qq + 1) * qc]

    def d4_group(g, first_group):
        offs = [_aligned((g * (DSWA_GROUP // 4) + i) * qc, qc) for i in range(DSWA_GROUP // 4)]
        res = [(r4, a, d4_tile(r4, a, first_group and i == 0))
               for i, a in enumerate(offs) for r4 in range(4)]
        for r4, a, val in res:
            d4_store(r4, a, val)

    d4_group(0, True)
    lax.fori_loop(1, U // qc // (DSWA_GROUP // 4), lambda g, c: (d4_group(g, False), c)[1], 0)

    for src, dst in ((mx, mn), (lx, ln), (ax, an)):
        for r in range(MAX_DIL):
            x4[r % 4, pl.ds(r // 4, U, stride=4), :] = src[r]
        for r4 in range(4):
            dst[pl.ds(r4, 4 * U, stride=4), :] = x4[r4]

    def d1_tile(n, first):
        qs = pl.ds(_aligned(n * QB, QB), QB)
        if first:
            ks, bias = qs, bias_first
        else:
            ks, bias = pl.ds(_aligned((n - 1) * QB, QB), 2 * QB), bias_band
        prev = (mn[qs, :], ln[qs, :], an[qs, :])
        _, l_new, acc_new = _softmax_block(q_ref[qs, :].astype(BF16), k_ref[ks, :].astype(BF16),
                                           v_ref[ks, :].astype(BF16), bias, prev)
        return qs, (acc_new / l_new).astype(o_ref.dtype)

    def d1_group(g, first_group):
        res = [d1_tile(g * DSWA_GROUP + j, first_group and j == 0) for j in range(DSWA_GROUP)]
        for qs, val in res:
            o_ref[qs, :] = val

    d1_group(0, True)
    lax.fori_loop(1, seq // QB // DSWA_GROUP, lambda g, c: (d1_group(g, False), c)[1], 0)


def _dswa(aq, ak, av, *, n_heads):
    B, S, _ = aq.shape
    U = S // MAX_DIL
    assert DILATIONS == (1, 4, MAX_DIL) and MAX_DIL % DSWA_GROUP == 0
    assert U % Q_BLOCK == 0 and (S // Q_BLOCK) % DSWA_GROUP == 0
    spec = pl.BlockSpec((None, S, HEAD_DIM), lambda b, h: (b, 0, h))
    x16 = lambda dt: pltpu.VMEM((MAX_DIL, U, LANES), dt)
    return pl.pallas_call(
        functools.partial(_dswa_kernel, seq=S),
        grid=(B, n_heads),
        in_specs=[spec, spec, spec],
        out_specs=spec,
        out_shape=jax.ShapeDtypeStruct(aq.shape, BF16),
        scratch_shapes=([x16(BF16)] * 3 + [x16(F32)] * 3 + [pltpu.VMEM((S, LANES), F32)] * 3
                        + [pltpu.VMEM((4, S // 4, LANES), F32)]),
        compiler_params=pltpu.CompilerParams(
            dimension_semantics=("parallel", "parallel"), vmem_limit_bytes=VMEM_LIMIT),
        name="dswa",
    )(aq, ak, av)


MLA_CHAIN = 256


def _cast_specs(weights, n_steps, index):
    specs, shapes = [], []
    for w in weights:
        rows = w.shape[0] // n_steps
        assert rows * n_steps == w.shape[0] and rows % 16 == 0
        specs.append(pl.BlockSpec((rows, w.shape[1]), index))
        shapes.append(jax.ShapeDtypeStruct(w.shape, BF16))
    return specs, shapes


def _mla_attn_kernel(q_ref, k_ref, v_ref, *rest, seq, tk):
    n_cast = (len(rest) - 1) // 2
    o_ref = rest[n_cast]
    for src, dst in zip(rest[:n_cast], rest[n_cast + 1:]):
        dst[...] = src[...].astype(BF16)
    _mla_attn_body(q_ref, k_ref, v_ref, o_ref, seq=seq, tk=tk)


def _mla_attn_body(q_ref, k_ref, v_ref, o_ref, *, seq, tk):
    cq = MLA_CHAIN
    n_chains = seq // cq
    ri = lax.broadcasted_iota(jnp.int32, (cq, cq), 0)
    ci = lax.broadcasted_iota(jnp.int32, (cq, cq), 1)
    diag_bias = jnp.where(ci <= ri, 0.0, NEG_INF).astype(F32)

    plans = []
    for c in range(n_chains):
        n_full = (c * cq) // tk
        plan = [(j * tk, tk, None) for j in range(n_full)]
        plan.append((n_full * tk, (c + 1) * cq - n_full * tk, diag_bias))
        plans.append(plan)

    state = [None] * n_chains
    for t in range(max(len(p) for p in plans)):
        for c in reversed(range(n_chains)):
            if t >= len(plans[c]):
                continue
            start, nk, bias = plans[c][t]
            rows = slice(c * cq, (c + 1) * cq)
            ks = slice(start, start + nk)
            state[c] = _softmax_block(q_ref[rows, :], k_ref[ks, :], v_ref[ks, :], bias, state[c])
            if t == len(plans[c]) - 1:
                _, l, acc = state[c]
                o_ref[rows, :] = (acc / l).astype(o_ref.dtype)


def _mla_attn(q, k, v, cast_weights, *, n_heads, tk):
    B, S, _ = q.shape
    assert tk % MLA_CHAIN == 0 and S % tk == 0
    head = lambda width: pl.BlockSpec((None, S, width), lambda b, h: (b, 0, h))
    cast_specs, cast_shapes = _cast_specs(cast_weights, B * n_heads,
                                          lambda b, h: (b * n_heads + h, 0))
    return pl.pallas_call(
        functools.partial(_mla_attn_kernel, seq=S, tk=tk),
        grid=(B, n_heads),
        in_specs=[head(MLA_QK_PAD), head(MLA_QK_PAD), head(V_HEAD_DIM)] + cast_specs,
        out_specs=[head(V_HEAD_DIM)] + cast_specs,
        out_shape=[jax.ShapeDtypeStruct((B, S, n_heads * V_HEAD_DIM), BF16)] + cast_shapes,
        compiler_params=pltpu.CompilerParams(
            dimension_semantics=("parallel", "parallel"), vmem_limit_bytes=VMEM_LIMIT),
        name="mla_attn",
    )(q, k, v, *cast_weights)


def _outproj_kernel(a_ref, b_ref, x_ref, w_ref, g_ref, o_ref, *, a_width):
    tm = x_ref.shape[0]
    for part in range(ROW_SPLIT):
        rows = slice(part * tm // ROW_SPLIT, (part + 1) * tm // ROW_SPLIT)
        y = jnp.dot(a_ref[rows, :], w_ref[0:a_width, :], preferred_element_type=F32)
        y = y + jnp.dot(b_ref[rows, :], w_ref[a_width:, :], preferred_element_type=F32)
        o_ref[rows, :] = x_ref[rows, :] + _rms(y, g_ref[...])


def _outproj(a, b, x2d, w_out, g_post, *, tm):
    T, D = x2d.shape
    const = lambda i: (0, 0)
    row = lambda i: (i, 0)
    return pl.pallas_call(
        functools.partial(_outproj_kernel, a_width=a.shape[1]),
        grid=(T // tm,),
        in_specs=[
            pl.BlockSpec((tm, a.shape[1]), row),
            pl.BlockSpec((tm, b.shape[1]), row),
            pl.BlockSpec((tm, D), row),
            pl.BlockSpec(w_out.shape, const, pipeline_mode=pl.Buffered(1)),
            pl.BlockSpec((1, D), const),
        ],
        out_specs=pl.BlockSpec((tm, D), row),
        out_shape=jax.ShapeDtypeStruct((T, D), F32),
        compiler_params=pltpu.CompilerParams(
            dimension_semantics=("parallel",), vmem_limit_bytes=VMEM_LIMIT),
        name="outproj",
    )(a, b, x2d, w_out, g_post)


def _mlp_kernel(x_ref, gpre_ref, gpost_ref, wu_ref, wd_ref, o_ref, h_sc):
    f = pl.program_id(1)
    last = pl.num_programs(1) - 1
    tm = x_ref.shape[0]

    def step(first, final):
        for part in range(ROW_SPLIT):
            rows = slice(part * tm // ROW_SPLIT, (part + 1) * tm // ROW_SPLIT)
            if first:
                h = _rms(x_ref[rows, :], gpre_ref[...]).astype(BF16)
                h_sc[rows, :] = h
            else:
                h = h_sc[rows, :]
            u = jnp.dot(h, wu_ref[...], preferred_element_type=F32)
            u = jnp.square(jnp.maximum(u, 0.0)).astype(BF16)
            y = jnp.dot(u, wd_ref[...], preferred_element_type=F32)
            acc = y if first else o_ref[rows, :] + y
            if final:
                o_ref[rows, :] = x_ref[rows, :] + _rms(acc, gpost_ref[...])
            else:
                o_ref[rows, :] = acc

    pl.when(f == 0)(lambda: step(True, False))
    pl.when(jnp.logical_and(f > 0, f < last))(lambda: step(False, False))
    pl.when(f == last)(lambda: step(False, True))


def _mlp(x2d, g_pre, g_post, w_up, w_down, *, tm, tf):
    T, D = x2d.shape
    F = w_up.shape[1]
    return pl.pallas_call(
        _mlp_kernel,
        grid=(T // tm, F // tf),
        in_specs=[
            pl.BlockSpec((tm, D), lambda i, f: (i, 0)),
            pl.BlockSpec((1, D), lambda i, f: (0, 0)),
            pl.BlockSpec((1, D), lambda i, f: (0, 0)),
            pl.BlockSpec((D, tf), lambda i, f: (0, f)),
            pl.BlockSpec((tf, D), lambda i, f: (f, 0)),
        ],
        out_specs=pl.BlockSpec((tm, D), lambda i, f: (i, 0)),
        out_shape=jax.ShapeDtypeStruct((T, D), F32),
        scratch_shapes=[pltpu.VMEM((tm, D), BF16)],
        compiler_params=pltpu.CompilerParams(
            dimension_semantics=("parallel", "arbitrary"), vmem_limit_bytes=VMEM_LIMIT),
        name="mlp",
    )(x2d, g_pre, g_post, w_up, w_down)


def _inv_freq_lanes(rot_dim):
    inv = ROPE_THETA ** (-jnp.arange(0, rot_dim, 2, dtype=F32) / rot_dim)
    lanes = jnp.concatenate([inv, inv, jnp.zeros((LANES - rot_dim,), F32)])
    return lanes.reshape(1, LANES)


def _layer(x2d, pos, B, S, norm_attn_pre, norm_attn_post, w_in, q_latent_norm, kv_latent_norm,
           w_uq, w_ukv, w_out, norm_mlp_pre, norm_mlp_post, w_up, w_down):
    D = x2d.shape[1]
    q_rank = q_latent_norm.shape[0]
    kv_rank = kv_latent_norm.shape[0]
    n_mla = w_uq.shape[1] // (QK_NOPE_DIM + QK_ROPE_DIM)
    a_width = (w_in.shape[1] - q_rank - kv_rank - QK_ROPE_DIM) // 3
    n_a = a_width // HEAD_DIM

    w_in_p = jnp.pad(w_in.astype(BF16), ((0, 0), (0, LANES - QK_ROPE_DIM)))
    wq3 = w_uq.reshape(q_rank, n_mla, QK_NOPE_DIM + QK_ROPE_DIM)
    wq_p = jnp.pad(wq3, ((0, 0), (0, 0), (0, MLA_QK_PAD - QK_NOPE_DIM - QK_ROPE_DIM)))
    wq_p = wq_p.reshape(q_rank, n_mla * MLA_QK_PAD).astype(BF16)
    wkv3 = w_ukv.reshape(kv_rank, n_mla, QK_NOPE_DIM + V_HEAD_DIM)
    wk_p = wkv3[:, :, :QK_NOPE_DIM].reshape(kv_rank, n_mla * QK_NOPE_DIM).astype(BF16)
    wv_p = wkv3[:, :, QK_NOPE_DIM:].reshape(kv_rank, n_mla * V_HEAD_DIM).astype(BF16)

    invfa = _inv_freq_lanes(ROT_DIM)
    invfb = _inv_freq_lanes(QK_ROPE_DIM)
    row = lambda g: g.reshape(1, -1)

    aq, ak, av, cq, ckv, kr = _inproj(
        x2d, pos, row(norm_attn_pre), w_in_p, invfa, invfb, row(q_latent_norm),
        row(kv_latent_norm), a_width=a_width, q_rank=q_rank, kv_rank=kv_rank, tm=512)
    q_b, k_b, v_b = _mla_up(cq, ckv, kr, pos, invfb, wq_p, wk_p, wv_p, n_heads=n_mla, tm=512)

    a_out = _dswa(aq.reshape(B, S, a_width), ak.reshape(B, S, a_width),
                  av.reshape(B, S, a_width), n_heads=n_a)
    b_out, w_out_b, w_up_b, w_down_b = _mla_attn(
        q_b.reshape(B, S, -1), k_b.reshape(B, S, -1), v_b.reshape(B, S, -1),
        [w_out, w_up, w_down], n_heads=n_mla, tk=512)

    x2d = _outproj(a_out.reshape(B * S, -1), b_out.reshape(B * S, -1), x2d,
                   w_out_b, row(norm_attn_post), tm=512)
    return _mlp(x2d, row(norm_mlp_pre), row(norm_mlp_post), w_up_b, w_down_b, tm=512, tf=2048)


def kernel(x, positions, norm_attn_pre, norm_attn_post, w_in, q_latent_norm, kv_latent_norm,
           w_uq, w_ukv, w_out, norm_mlp_pre, norm_mlp_post, w_up, w_down):
    B, S, D = x.shape
    x2d = x.reshape(B * S, D)
    pos = positions.astype(F32).reshape(B * S, 1)
    for layer in range(w_in.shape[0]):
        x2d = _layer(x2d, pos, B, S, norm_attn_pre[layer], norm_attn_post[layer], w_in[layer],
                     q_latent_norm[layer], kv_latent_norm[layer], w_uq[layer], w_ukv[layer],
                     w_out[layer], norm_mlp_pre[layer], norm_mlp_post[layer], w_up[layer],
                     w_down[layer])
    return x2d.reshape(B, S, D)
```

```python
import functools

import jax
import jax.numpy as jnp
from jax import lax
from jax.experimental import pallas as pl
from jax.experimental.pallas import tpu as pltpu

F32 = jnp.float32
BF16 = jnp.bfloat16

LANES = 128
HEAD_DIM = 128
ROT_DIM = HEAD_DIM // 4
ROPE_THETA = 500000.0
QK_NOPE_DIM = 128
QK_ROPE_DIM = 64
V_HEAD_DIM = 128
MLA_QK_PAD = 256
Q_BLOCK = 128
DILATIONS = (1, 4, 16)
NORM_EPS = 1e-6
NEG_INF = -1e30
LOG2E = 1.4426950408889634
VMEM_LIMIT = 56 * 1024 * 1024


def _rms(x, gain):
    ms = jnp.mean(x * x, axis=-1, keepdims=True)
    return x * lax.rsqrt(ms + NORM_EPS) * gain


def _rope_tables(pos, inv_freq, half):
    lane = lax.broadcasted_iota(jnp.int32, (1, LANES), 1)
    ang = pos * inv_freq
    sign = jnp.where(lane < half, -1.0, 1.0).astype(F32)
    return jnp.cos(ang), jnp.sin(ang) * sign


def _rope(r, cos, sin, half):
    lane = lax.broadcasted_iota(jnp.int32, (1, LANES), 1)
    partner = jnp.where(lane < half,
                        pltpu.roll(r, LANES - half, 1),
                        pltpu.roll(r, half, 1))
    return r * cos + partner * sin


ROW_SPLIT = 2


def _inproj_kernel(x_ref, pos_ref, g_ref, w_ref, invfa_ref, invfb_ref, gq_ref, gkv_ref,
                   aq_ref, ak_ref, av_ref, cq_ref, ckv_ref, kr_ref, *, a_width, q_rank, kv_rank):
    n_heads = a_width // HEAD_DIM
    q_scale = HEAD_DIM ** -0.5 * LOG2E
    tm = x_ref.shape[0]
    for part in range(ROW_SPLIT):
        rows = slice(part * tm // ROW_SPLIT, (part + 1) * tm // ROW_SPLIT)
        h = _rms(x_ref[rows, :], g_ref[...]).astype(BF16)
        pos = pos_ref[rows, :]
        cos_a, sin_a = _rope_tables(pos, invfa_ref[...], ROT_DIM // 2)
        cos_b, sin_b = _rope_tables(pos, invfb_ref[...], QK_ROPE_DIM // 2)

        r = jnp.dot(h, w_ref[:, 0:a_width], preferred_element_type=F32)
        for hd in range(n_heads):
            sl = slice(hd * HEAD_DIM, (hd + 1) * HEAD_DIM)
            aq_ref[hd, rows, :] = _rope(r[:, sl], cos_a, sin_a, ROT_DIM // 2) * q_scale
        r = jnp.dot(h, w_ref[:, a_width:2 * a_width], preferred_element_type=F32)
        for hd in range(n_heads):
            sl = slice(hd * HEAD_DIM, (hd + 1) * HEAD_DIM)
            ak_ref[hd, rows, :] = _rope(r[:, sl], cos_a, sin_a, ROT_DIM // 2)
        r = jnp.dot(h, w_ref[:, 2 * a_width:3 * a_width], preferred_element_type=F32)
        for hd in range(n_heads):
            av_ref[hd, rows, :] = r[:, hd * HEAD_DIM:(hd + 1) * HEAD_DIM]

        r = jnp.dot(h, w_ref[:, 3 * a_width:], preferred_element_type=F32)
        cq_ref[rows, :] = _rms(r[:, 0:q_rank], gq_ref[...]).astype(BF16)
        ckv_ref[rows, :] = _rms(r[:, q_rank:q_rank + kv_rank], gkv_ref[...]).astype(BF16)
        kr = r[:, q_rank + kv_rank:q_rank + kv_rank + LANES]
        kr_ref[rows, :] = _rope(kr, cos_b, sin_b, QK_ROPE_DIM // 2).astype(BF16)


def _inproj(x2d, pos, g_pre, w_in, invfa, invfb, gq, gkv, *, batch, a_width, q_rank, kv_rank, tm):
    T, D = x2d.shape
    S = T // batch
    assert S % tm == 0
    tiles = S // tm
    n_heads = a_width // HEAD_DIM
    ncols = w_in.shape[1]
    const = lambda i: (0, 0)
    row = lambda i: (i, 0)
    head_major = pl.BlockSpec((None, n_heads, tm, HEAD_DIM),
                              lambda i: (i // tiles, 0, i % tiles, 0))
    head_shape = jax.ShapeDtypeStruct((batch, n_heads, S, HEAD_DIM), F32)
    kern = functools.partial(_inproj_kernel, a_width=a_width, q_rank=q_rank, kv_rank=kv_rank)
    return pl.pallas_call(
        kern,
        grid=(T // tm,),
        in_specs=[
            pl.BlockSpec((tm, D), row),
            pl.BlockSpec((tm, 1), row),
            pl.BlockSpec((1, D), const),
            pl.BlockSpec((D, ncols), const, pipeline_mode=pl.Buffered(1)),
            pl.BlockSpec((1, LANES), const),
            pl.BlockSpec((1, LANES), const),
            pl.BlockSpec((1, q_rank), const),
            pl.BlockSpec((1, kv_rank), const),
        ],
        out_specs=[
            head_major, head_major, head_major,
            pl.BlockSpec((tm, q_rank), row),
            pl.BlockSpec((tm, kv_rank), row),
            pl.BlockSpec((tm, LANES), row),
        ],
        out_shape=[
            head_shape, head_shape, head_shape,
            jax.ShapeDtypeStruct((T, q_rank), BF16),
            jax.ShapeDtypeStruct((T, kv_rank), BF16),
            jax.ShapeDtypeStruct((T, LANES), BF16),
        ],
        compiler_params=pltpu.CompilerParams(
            dimension_semantics=("parallel",), vmem_limit_bytes=VMEM_LIMIT),
        name="inproj",
    )(x2d, pos, g_pre, w_in, invfa, invfb, gq, gkv)


def _mla_up_kernel(cq_ref, ckv_ref, kr_ref, pos_ref, invfb_ref, wq_ref, wk_ref, wv_ref,
                   q_ref, k_ref, v_ref, *, n_heads):
    scale = (QK_NOPE_DIM + QK_ROPE_DIM) ** -0.5 * LOG2E
    tm = cq_ref.shape[0]
    for part in range(ROW_SPLIT):
        rows = slice(part * tm // ROW_SPLIT, (part + 1) * tm // ROW_SPLIT)
        cos_b, sin_b = _rope_tables(pos_ref[rows, :], invfb_ref[...], QK_ROPE_DIM // 2)
        rq = jnp.dot(cq_ref[rows, :], wq_ref[...], preferred_element_type=F32)
        ckv = ckv_ref[rows, :]
        rk = jnp.dot(ckv, wk_ref[...], preferred_element_type=F32)
        kr = kr_ref[rows, :]
        rv = jnp.dot(ckv, wv_ref[...], preferred_element_type=F32)
        for hd in range(n_heads):
            base = hd * MLA_QK_PAD
            q_ref[hd, rows, 0:LANES] = (rq[:, base:base + LANES] * scale).astype(BF16)
            q_rot = _rope(rq[:, base + LANES:base + 2 * LANES], cos_b, sin_b, QK_ROPE_DIM // 2)
            q_ref[hd, rows, LANES:2 * LANES] = (q_rot * scale).astype(BF16)
            k_ref[hd, rows, 0:LANES] = rk[:, hd * LANES:(hd + 1) * LANES].astype(BF16)
            k_ref[hd, rows, LANES:2 * LANES] = kr
            v_ref[hd, rows, :] = rv[:, hd * V_HEAD_DIM:(hd + 1) * V_HEAD_DIM].astype(BF16)


def _mla_up(cq, ckv, kr, pos, invfb, wq, wk, wv, *, batch, n_heads, tm):
    T = cq.shape[0]
    S = T // batch
    assert S % tm == 0
    tiles = S // tm
    const = lambda i: (0, 0)
    row = lambda i: (i, 0)
    head_major = lambda width: pl.BlockSpec((None, n_heads, tm, width),
                                            lambda i: (i // tiles, 0, i % tiles, 0))
    head_shape = lambda width: jax.ShapeDtypeStruct((batch, n_heads, S, width), BF16)
    return pl.pallas_call(
        functools.partial(_mla_up_kernel, n_heads=n_heads),
        grid=(T // tm,),
        in_specs=[
            pl.BlockSpec((tm, cq.shape[1]), row),
            pl.BlockSpec((tm, ckv.shape[1]), row),
            pl.BlockSpec((tm, LANES), row),
            pl.BlockSpec((tm, 1), row),
            pl.BlockSpec((1, LANES), const),
            pl.BlockSpec(wq.shape, const),
            pl.BlockSpec(wk.shape, const),
            pl.BlockSpec(wv.shape, const),
        ],
        out_specs=[head_major(MLA_QK_PAD), head_major(MLA_QK_PAD), head_major(V_HEAD_DIM)],
        out_shape=[head_shape(MLA_QK_PAD), head_shape(MLA_QK_PAD), head_shape(V_HEAD_DIM)],
        compiler_params=pltpu.CompilerParams(
            dimension_semantics=("parallel",), vmem_limit_bytes=VMEM_LIMIT),
        name="mla_up",
    )(cq, ckv, kr, pos, invfb, wq, wk, wv)


DSWA_GROUP = 16
MAX_DIL = 16
NT_DIMS = (((1,), (1,)), ((), ()))


def _aligned(x, multiple):
    return x if isinstance(x, int) else pl.multiple_of(x, multiple)


def _softmax_block(q, k, v, bias, prev):
    nk = k.shape[0]
    s = lax.dot_general(q, k, NT_DIMS, preferred_element_type=F32)
    blocks = [s[:, i * LANES:(i + 1) * LANES] for i in range(nk // LANES)]
    if bias is not None:
        nb = bias.shape[1] // LANES
        blocks[-nb:] = [b + bias[:, i * LANES:(i + 1) * LANES]
                        for i, b in enumerate(blocks[-nb:])]
    m_cur = jnp.max(functools.reduce(jnp.maximum, blocks), axis=1, keepdims=True)
    if prev is None:
        m_new = jnp.broadcast_to(m_cur, (q.shape[0], LANES))
    else:
        m_prev, l_prev, acc_prev = prev
        m_new = jnp.maximum(m_prev, m_cur)
        alpha = jnp.exp2(m_prev - m_new)
    p = jnp.concatenate([jnp.exp2(b - m_new) for b in blocks], axis=1).astype(BF16)
    v_aug = jnp.concatenate([v, jnp.ones((nk, LANES), BF16)], axis=1)
    pv = jnp.dot(p, v_aug, preferred_element_type=F32)
    acc_cur, l_cur = pv[:, :LANES], pv[:, LANES:]
    if prev is None:
        return m_new, l_cur, acc_cur
    return m_new, alpha * l_prev + l_cur, alpha * acc_prev + acc_cur


def _dswa_kernel(q_ref, k_ref, v_ref, o_ref, xq, xk, xv, mx, lx, ax, mn, ln, an, x4, *, seq):
    U = seq // MAX_DIL
    QB = Q_BLOCK
    NEG = NEG_INF

    def band_bias(dist):
        return jnp.where(dist >= 0, jnp.where(dist <= QB, 0.0, NEG), NEG).astype(F32)

    i1 = lax.broadcasted_iota(jnp.int32, (QB, QB), 0)
    j1 = lax.broadcasted_iota(jnp.int32, (QB, QB), 1)
    i2 = lax.broadcasted_iota(jnp.int32, (QB, 2 * QB), 0)
    j2 = lax.broadcasted_iota(jnp.int32, (QB, 2 * QB), 1)
    bias_first = band_bias(i1 - j1)
    bias_band = band_bias(i2 + QB - j2)
    qc = QB // 4
    lo = lambda x, n: x & (n - 1)
    hi = lambda x, n: x >> (n.bit_length() - 1)
    d4_first = band_bias(4 * (lo(i1, qc) - lo(j1, qc)) + hi(i1, qc) - hi(j1, qc))
    d4_band = band_bias(4 * (lo(i2, qc) - lo(j2, 2 * qc) + qc) + hi(i2, qc) - hi(j2, 2 * qc))

    for src, dst in ((q_ref, xq), (k_ref, xk), (v_ref, xv)):
        for r4 in range(4):
            x4[r4] = src[pl.ds(r4, 4 * U, stride=4), :]
        for r in range(MAX_DIL):
            dst[r] = x4[r % 4, pl.ds(r // 4, U, stride=4), :].astype(BF16)

    def d16_group(g, carry):
        out = []
        for j in range(DSWA_GROUP):
            r = g * DSWA_GROUP + j
            out.append((r, 0, _softmax_block(xq[r, 0:QB, :], xk[r, 0:QB, :], xv[r, 0:QB, :],
                                             bias_first, None)))
            for n in range(1, U // QB):
                ks = slice((n - 1) * QB, (n + 1) * QB)
                out.append((r, n, _softmax_block(xq[r, n * QB:(n + 1) * QB, :], xk[r, ks, :],
                                                 xv[r, ks, :], bias_band, None)))
        for r, n, (m_new, l_new, acc_new) in out:
            qs = slice(n * QB, (n + 1) * QB)
            mx[r, qs, :] = m_new
            lx[r, qs, :] = l_new
            ax[r, qs, :] = acc_new
        return carry
    lax.fori_loop(0, MAX_DIL // DSWA_GROUP, d16_group, 0)

    def d4_tile(r4, a, first):
        def gather(ref, start, size):
            return jnp.concatenate([ref[4 * qq + r4, pl.ds(start, size), :] for qq in range(4)],
                                   axis=0)
        q = gather(xq, a, qc)
        prev = (gather(mx, a, qc), gather(lx, a, qc), gather(ax, a, qc))
        if first:
            k, v, bias = gather(xk, a, qc), gather(xv, a, qc), d4_first
        else:
            k, v, bias = gather(xk, a - qc, 2 * qc), gather(xv, a - qc, 2 * qc), d4_band
        return _softmax_block(q, k, v, bias, prev)

    def d4_store(r4, a, res):
        for ref, val in zip((mx, lx, ax), res):
            for qq in range(4):
                ref[4 * qq + r4, pl.ds(a, qc), :] = val[qq * qc:(qq + 1) * qc]

    def d4_group(g, first_group):
        offs = [_aligned((g * (DSWA_GROUP // 4) + i) * qc, qc) for i in range(DSWA_GROUP // 4)]
        res = [(r4, a, d4_tile(r4, a, first_group and i == 0))
               for i, a in enumerate(offs) for r4 in range(4)]
        for r4, a, val in res:
            d4_store(r4, a, val)

    d4_group(0, True)
    lax.fori_loop(1, U // qc // (DSWA_GROUP // 4), lambda g, c: (d4_group(g, False), c)[1], 0)

    for src, dst in ((mx, mn), (lx, ln), (ax, an)):
        for r in range(MAX_DIL):
            x4[r % 4, pl.ds(r // 4, U, stride=4), :] = src[r]
        for r4 in range(4):
            dst[pl.ds(r4, 4 * U, stride=4), :] = x4[r4]

    def d1_tile(n, first):
        qs = pl.ds(_aligned(n * QB, QB), QB)
        if first:
            ks, bias = qs, bias_first
        else:
            ks, bias = pl.ds(_aligned((n - 1) * QB, QB), 2 * QB), bias_band
        prev = (mn[qs, :], ln[qs, :], an[qs, :])
        _, l_new, acc_new = _softmax_block(q_ref[qs, :].astype(BF16), k_ref[ks, :].astype(BF16),
                                           v_ref[ks, :].astype(BF16), bias, prev)
        return qs, (acc_new / l_new).astype(o_ref.dtype)

    def d1_group(g, first_group):
        res = [d1_tile(g * DSWA_GROUP + j, first_group and j == 0) for j in range(DSWA_GROUP)]
        for qs, val in res:
            o_ref[qs, :] = val

    d1_group(0, True)
    lax.fori_loop(1, seq // QB // DSWA_GROUP, lambda g, c: (d1_group(g, False), c)[1], 0)


def _dswa(aq, ak, av):
    B, n_heads, S, _ = aq.shape
    U = S // MAX_DIL
    assert DILATIONS == (1, 4, MAX_DIL) and MAX_DIL % DSWA_GROUP == 0
    assert U % Q_BLOCK == 0 and (S // Q_BLOCK) % DSWA_GROUP == 0
    spec = pl.BlockSpec((None, None, S, HEAD_DIM), lambda b, h: (b, h, 0, 0))
    x16 = lambda dt: pltpu.VMEM((MAX_DIL, U, LANES), dt)
    return pl.pallas_call(
        functools.partial(_dswa_kernel, seq=S),
        grid=(B, n_heads),
        in_specs=[spec, spec, spec],
        out_specs=pl.BlockSpec((None, S, HEAD_DIM), lambda b, h: (b, 0, h)),
        out_shape=jax.ShapeDtypeStruct((B, S, n_heads * HEAD_DIM), BF16),
        scratch_shapes=([x16(BF16)] * 3 + [x16(F32)] * 3 + [pltpu.VMEM((S, LANES), F32)] * 3
                        + [pltpu.VMEM((4, S // 4, LANES), F32)]),
        compiler_params=pltpu.CompilerParams(
            dimension_semantics=("parallel", "parallel"), vmem_limit_bytes=VMEM_LIMIT),
        name="dswa",
    )(aq, ak, av)


MLA_CHAIN = 256


def _cast_specs(weights, n_steps, index):
    specs, shapes = [], []
    for w in weights:
        rows = w.shape[0] // n_steps
        assert rows * n_steps == w.shape[0] and rows % 16 == 0
        specs.append(pl.BlockSpec((rows, w.shape[1]), index))
        shapes.append(jax.ShapeDtypeStruct(w.shape, BF16))
    return specs, shapes


def _mla_attn_kernel(q_ref, k_ref, v_ref, *rest, seq, tk):
    n_cast = (len(rest) - 1) // 2
    o_ref = rest[n_cast]
    for src, dst in zip(rest[:n_cast], rest[n_cast + 1:]):
        dst[...] = src[...].astype(BF16)
    _mla_attn_body(q_ref, k_ref, v_ref, o_ref, seq=seq, tk=tk)


def _mla_attn_body(q_ref, k_ref, v_ref, o_ref, *, seq, tk):
    cq = MLA_CHAIN
    n_chains = seq // cq
    ri = lax.broadcasted_iota(jnp.int32, (cq, cq), 0)
    ci = lax.broadcasted_iota(jnp.int32, (cq, cq), 1)
    diag_bias = jnp.where(ci <= ri, 0.0, NEG_INF).astype(F32)

    plans = []
    for c in range(n_chains):
        n_full = (c * cq) // tk
        plan = [(j * tk, tk, None) for j in range(n_full)]
        plan.append((n_full * tk, (c + 1) * cq - n_full * tk, diag_bias))
        plans.append(plan)

    state = [None] * n_chains
    for t in range(max(len(p) for p in plans)):
        for c in reversed(range(n_chains)):
            if t >= len(plans[c]):
                continue
            start, nk, bias = plans[c][t]
            rows = slice(c * cq, (c + 1) * cq)
            ks = slice(start, start + nk)
            state[c] = _softmax_block(q_ref[rows, :], k_ref[ks, :], v_ref[ks, :], bias, state[c])
            if t == len(plans[c]) - 1:
                _, l, acc = state[c]
                o_ref[rows, :] = (acc / l).astype(o_ref.dtype)


def _mla_attn(q, k, v, cast_weights, *, tk):
    B, n_heads, S, _ = q.shape
    assert tk % MLA_CHAIN == 0 and S % tk == 0
    head = lambda width: pl.BlockSpec((None, None, S, width), lambda b, h: (b, h, 0, 0))
    cast_specs, cast_shapes = _cast_specs(cast_weights, B * n_heads,
                                          lambda b, h: (b * n_heads + h, 0))
    return pl.pallas_call(
        functools.partial(_mla_attn_kernel, seq=S, tk=tk),
        grid=(B, n_heads),
        in_specs=[head(MLA_QK_PAD), head(MLA_QK_PAD), head(V_HEAD_DIM)] + cast_specs,
        out_specs=[pl.BlockSpec((None, S, V_HEAD_DIM), lambda b, h: (b, 0, h))] + cast_specs,
        out_shape=[jax.ShapeDtypeStruct((B, S, n_heads * V_HEAD_DIM), BF16)] + cast_shapes,
        compiler_params=pltpu.CompilerParams(
            dimension_semantics=("parallel", "parallel"), vmem_limit_bytes=VMEM_LIMIT),
        name="mla_attn",
    )(q, k, v, *cast_weights)


def _outproj_kernel(a_ref, b_ref, x_ref, w_ref, g_ref, o_ref, *, a_width):
    tm = x_ref.shape[0]
    for part in range(ROW_SPLIT):
        rows = slice(part * tm // ROW_SPLIT, (part + 1) * tm // ROW_SPLIT)
        y = jnp.dot(a_ref[rows, :], w_ref[0:a_width, :], preferred_element_type=F32)
        y = y + jnp.dot(b_ref[rows, :], w_ref[a_width:, :], preferred_element_type=F32)
        o_ref[rows, :] = x_ref[rows, :] + _rms(y, g_ref[...])


def _outproj(a, b, x2d, w_out, g_post, *, tm):
    T, D = x2d.shape
    const = lambda i: (0, 0)
    row = lambda i: (i, 0)
    return pl.pallas_call(
        functools.partial(_outproj_kernel, a_width=a.shape[1]),
        grid=(T // tm,),
        in_specs=[
            pl.BlockSpec((tm, a.shape[1]), row),
            pl.BlockSpec((tm, b.shape[1]), row),
            pl.BlockSpec((tm, D), row),
            pl.BlockSpec(w_out.shape, const, pipeline_mode=pl.Buffered(1)),
            pl.BlockSpec((1, D), const),
        ],
        out_specs=pl.BlockSpec((tm, D), row),
        out_shape=jax.ShapeDtypeStruct((T, D), F32),
        compiler_params=pltpu.CompilerParams(
            dimension_semantics=("parallel",), vmem_limit_bytes=VMEM_LIMIT),
        name="outproj",
    )(a, b, x2d, w_out, g_post)


def _mlp_kernel(x_ref, gpre_ref, gpost_ref, wu_ref, wd_ref, o_ref, h_sc):
    f = pl.program_id(1)
    last = pl.num_programs(1) - 1
    tm = x_ref.shape[0]

    def step(first, final):
        for part in range(ROW_SPLIT):
            rows = slice(part * tm // ROW_SPLIT, (part + 1) * tm // ROW_SPLIT)
            if first:
                h = _rms(x_ref[rows, :], gpre_ref[...]).astype(BF16)
                h_sc[rows, :] = h
            else:
                h = h_sc[rows, :]
            u = jnp.dot(h, wu_ref[...], preferred_element_type=F32)
            u = jnp.square(jnp.maximum(u, 0.0)).astype(BF16)
            y = jnp.dot(u, wd_ref[...], preferred_element_type=F32)
            acc = y if first else o_ref[rows, :] + y
            if final:
                o_ref[rows, :] = x_ref[rows, :] + _rms(acc, gpost_ref[...])
            else:
                o_ref[rows, :] = acc

    pl.when(f == 0)(lambda: step(True, False))
    pl.when(jnp.logical_and(f > 0, f < last))(lambda: step(False, False))
    pl.when(f == last)(lambda: step(False, True))


def _mlp(x2d, g_pre, g_post, w_up, w_down, *, tm, tf):
    T, D = x2d.shape
    F = w_up.shape[1]
    return pl.pallas_call(
        _mlp_kernel,
        grid=(T // tm, F // tf),
        in_specs=[
            pl.BlockSpec((tm, D), lambda i, f: (i, 0)),
            pl.BlockSpec((1, D), lambda i, f: (0, 0)),
            pl.BlockSpec((1, D), lambda i, f: (0, 0)),
            pl.BlockSpec((D, tf), lambda i, f: (0, f)),
            pl.BlockSpec((tf, D), lambda i, f: (f, 0)),
        ],
        out_specs=pl.BlockSpec((tm, D), lambda i, f: (i, 0)),
        out_shape=jax.ShapeDtypeStruct((T, D), F32),
        scratch_shapes=[pltpu.VMEM((tm, D), BF16)],
        compiler_params=pltpu.CompilerParams(
            dimension_semantics=("parallel", "arbitrary"), vmem_limit_bytes=VMEM_LIMIT),
        name="mlp",
    )(x2d, g_pre, g_post, w_up, w_down)


def _inv_freq_lanes(rot_dim):
    inv = ROPE_THETA ** (-jnp.arange(0, rot_dim, 2, dtype=F32) / rot_dim)
    lanes = jnp.concatenate([inv, inv, jnp.zeros((LANES - rot_dim,), F32)])
    return lanes.reshape(1, LANES)


def _layer(x2d, pos, B, S, norm_attn_pre, norm_attn_post, w_in, q_latent_norm, kv_latent_norm,
           w_uq, w_ukv, w_out, norm_mlp_pre, norm_mlp_post, w_up, w_down):
    D = x2d.shape[1]
    q_rank = q_latent_norm.shape[0]
    kv_rank = kv_latent_norm.shape[0]
    n_mla = w_uq.shape[1] // (QK_NOPE_DIM + QK_ROPE_DIM)
    a_width = (w_in.shape[1] - q_rank - kv_rank - QK_ROPE_DIM) // 3

    w_in_p = jnp.concatenate(
        [w_in.astype(BF16), jnp.zeros((D, LANES - QK_ROPE_DIM), BF16)], axis=1)
    wq3 = w_uq.reshape(q_rank, n_mla, QK_NOPE_DIM + QK_ROPE_DIM)
    wq_p = jnp.pad(wq3, ((0, 0), (0, 0), (0, MLA_QK_PAD - QK_NOPE_DIM - QK_ROPE_DIM)))
    wq_p = wq_p.reshape(q_rank, n_mla * MLA_QK_PAD).astype(BF16)
    wkv3 = w_ukv.reshape(kv_rank, n_mla, QK_NOPE_DIM + V_HEAD_DIM)
    wk_p = wkv3[:, :, :QK_NOPE_DIM].reshape(kv_rank, n_mla * QK_NOPE_DIM).astype(BF16)
    wv_p = wkv3[:, :, QK_NOPE_DIM:].reshape(kv_rank, n_mla * V_HEAD_DIM).astype(BF16)

    invfa = _inv_freq_lanes(ROT_DIM)
    invfb = _inv_freq_lanes(QK_ROPE_DIM)
    row = lambda g: g.reshape(1, -1)

    aq, ak, av, cq, ckv, kr = _inproj(
        x2d, pos, row(norm_attn_pre), w_in_p, invfa, invfb, row(q_latent_norm),
        row(kv_latent_norm), batch=B, a_width=a_width, q_rank=q_rank, kv_rank=kv_rank, tm=512)
    q_b, k_b, v_b = _mla_up(cq, ckv, kr, pos, invfb, wq_p, wk_p, wv_p, batch=B, n_heads=n_mla,
                            tm=512)

    a_out = _dswa(aq, ak, av)
    b_out, w_out_b, w_up_b, w_down_b = _mla_attn(q_b, k_b, v_b, [w_out, w_up, w_down], tk=512)

    x2d = _outproj(a_out.reshape(B * S, -1), b_out.reshape(B * S, -1), x2d,
                   w_out_b, row(norm_attn_post), tm=512)
    return _mlp(x2d, row(norm_mlp_pre), row(norm_mlp_post), w_up_b, w_down_b, tm=512, tf=2048)


def kernel(x, positions, norm_attn_pre, norm_attn_post, w_in, q_latent_norm, kv_latent_norm,
           w_uq, w_ukv, w_out, norm_mlp_pre, norm_mlp_post, w_up, w_down):
    B, S, D = x.shape
    x2d = x.reshape(B * S, D)
    pos = positions.astype(F32).reshape(B * S, 1)
    for layer in range(w_in.shape[0]):
        x2d = _layer(x2d, pos, B, S, norm_attn_pre[layer], norm_attn_post[layer], w_in[layer],
                     q_latent_norm[layer], kv_latent_norm[layer], w_uq[layer], w_ukv[layer],
                     w_out[layer], norm_mlp_pre[layer], norm_mlp_post[layer], w_up[layer],
                     w_down[layer])
    return x2d.reshape(B, S, D)
```

```python
import functools

import jax
import jax.numpy as jnp
from jax import lax
from jax.experimental import pallas as pl
from jax.experimental.pallas import tpu as pltpu

F32 = jnp.float32
BF16 = jnp.bfloat16

LANES = 128
HEAD_DIM = 128
ROT_DIM = HEAD_DIM // 4
ROPE_THETA = 500000.0
QK_NOPE_DIM = 128
QK_ROPE_DIM = 64
V_HEAD_DIM = 128
MLA_QK_PAD = 256
Q_BLOCK = 128
DILATIONS = (1, 4, 16)
NORM_EPS = 1e-6
NEG_INF = -1e30
LOG2E = 1.4426950408889634
VMEM_LIMIT = 56 * 1024 * 1024


def _rms(x, gain):
    ms = jnp.mean(x * x, axis=-1, keepdims=True)
    return x * lax.rsqrt(ms + NORM_EPS) * gain


def _rope_tables(pos, inv_freq, half):
    lane = lax.broadcasted_iota(jnp.int32, (1, LANES), 1)
    ang = pos * inv_freq
    sign = jnp.where(lane < half, -1.0, 1.0).astype(F32)
    return jnp.cos(ang), jnp.sin(ang) * sign


def _rope(r, cos, sin, half):
    lane = lax.broadcasted_iota(jnp.int32, (1, LANES), 1)
    partner = jnp.where(lane < half,
                        pltpu.roll(r, LANES - half, 1),
                        pltpu.roll(r, half, 1))
    return r * cos + partner * sin


ROW_SPLIT = 2


def _inproj_kernel(x_ref, pos_ref, g_ref, w_ref, invfa_ref, invfb_ref, gq_ref, gkv_ref,
                   aq_ref, ak_ref, av_ref, cq_ref, ckv_ref, kr_ref, *, a_width, q_rank, kv_rank):
    n_heads = a_width // HEAD_DIM
    q_scale = HEAD_DIM ** -0.5 * LOG2E
    tm = x_ref.shape[0]
    for part in range(ROW_SPLIT):
        rows = slice(part * tm // ROW_SPLIT, (part + 1) * tm // ROW_SPLIT)
        h = _rms(x_ref[rows, :], g_ref[...]).astype(BF16)
        pos = pos_ref[rows, :]
        cos_a, sin_a = _rope_tables(pos, invfa_ref[...], ROT_DIM // 2)
        cos_b, sin_b = _rope_tables(pos, invfb_ref[...], QK_ROPE_DIM // 2)

        r = jnp.dot(h, w_ref[:, 0:a_width], preferred_element_type=F32)
        for hd in range(n_heads):
            sl = slice(hd * HEAD_DIM, (hd + 1) * HEAD_DIM)
            aq_ref[hd, rows, :] = _rope(r[:, sl], cos_a, sin_a, ROT_DIM // 2) * q_scale
        r = jnp.dot(h, w_ref[:, a_width:2 * a_width], preferred_element_type=F32)
        for hd in range(n_heads):
            sl = slice(hd * HEAD_DIM, (hd + 1) * HEAD_DIM)
            ak_ref[hd, rows, :] = _rope(r[:, sl], cos_a, sin_a, ROT_DIM // 2)
        r = jnp.dot(h, w_ref[:, 2 * a_width:3 * a_width], preferred_element_type=F32)
        for hd in range(n_heads):
            av_ref[hd, rows, :] = r[:, hd * HEAD_DIM:(hd + 1) * HEAD_DIM]

        r = jnp.dot(h, w_ref[:, 3 * a_width:], preferred_element_type=F32)
        cq_ref[rows, :] = _rms(r[:, 0:q_rank], gq_ref[...]).astype(BF16)
        ckv_ref[rows, :] = _rms(r[:, q_rank:q_rank + kv_rank], gkv_ref[...]).astype(BF16)
        kr = r[:, q_rank + kv_rank:q_rank + kv_rank + LANES]
        kr_ref[rows, :] = _rope(kr, cos_b, sin_b, QK_ROPE_DIM // 2).astype(BF16)


def _inproj(x2d, pos, g_pre, w_in, invfa, invfb, gq, gkv, *, batch, a_width, q_rank, kv_rank, tm):
    T, D = x2d.shape
    S = T // batch
    assert S % tm == 0
    tiles = S // tm
    n_heads = a_width // HEAD_DIM
    ncols = w_in.shape[1]
    const = lambda i: (0, 0)
    row = lambda i: (i, 0)
    head_major = pl.BlockSpec((None, n_heads, tm, HEAD_DIM),
                              lambda i: (i // tiles, 0, i % tiles, 0))
    head_shape = jax.ShapeDtypeStruct((batch, n_heads, S, HEAD_DIM), F32)
    kern = functools.partial(_inproj_kernel, a_width=a_width, q_rank=q_rank, kv_rank=kv_rank)
    return pl.pallas_call(
        kern,
        grid=(T // tm,),
        in_specs=[
            pl.BlockSpec((tm, D), row),
            pl.BlockSpec((tm, 1), row),
            pl.BlockSpec((1, D), const),
            pl.BlockSpec((D, ncols), const, pipeline_mode=pl.Buffered(1)),
            pl.BlockSpec((1, LANES), const),
            pl.BlockSpec((1, LANES), const),
            pl.BlockSpec((1, q_rank), const),
            pl.BlockSpec((1, kv_rank), const),
        ],
        out_specs=[
            head_major, head_major, head_major,
            pl.BlockSpec((tm, q_rank), row),
            pl.BlockSpec((tm, kv_rank), row),
            pl.BlockSpec((tm, LANES), row),
        ],
        out_shape=[
            head_shape, head_shape, head_shape,
            jax.ShapeDtypeStruct((T, q_rank), BF16),
            jax.ShapeDtypeStruct((T, kv_rank), BF16),
            jax.ShapeDtypeStruct((T, LANES), BF16),
        ],
        compiler_params=pltpu.CompilerParams(
            dimension_semantics=("parallel",), vmem_limit_bytes=VMEM_LIMIT),
        name="inproj",
    )(x2d, pos, g_pre, w_in, invfa, invfb, gq, gkv)


def _mla_up_kernel(cq_ref, ckv_ref, kr_ref, pos_ref, invfb_ref, wq_ref, wk_ref, wv_ref,
                   q_ref, k_ref, v_ref, *, n_heads):
    scale = (QK_NOPE_DIM + QK_ROPE_DIM) ** -0.5 * LOG2E
    tm = cq_ref.shape[0]
    for part in range(ROW_SPLIT):
        rows = slice(part * tm // ROW_SPLIT, (part + 1) * tm // ROW_SPLIT)
        cos_b, sin_b = _rope_tables(pos_ref[rows, :], invfb_ref[...], QK_ROPE_DIM // 2)
        rq = jnp.dot(cq_ref[rows, :], wq_ref[...], preferred_element_type=F32)
        ckv = ckv_ref[rows, :]
        rk = jnp.dot(ckv, wk_ref[...], preferred_element_type=F32)
        kr = kr_ref[rows, :]
        rv = jnp.dot(ckv, wv_ref[...], preferred_element_type=F32)
        for hd in range(n_heads):
            base = hd * MLA_QK_PAD
            q_ref[hd, rows, 0:LANES] = (rq[:, base:base + LANES] * scale).astype(BF16)
            q_rot = _rope(rq[:, base + LANES:base + 2 * LANES], cos_b, sin_b, QK_ROPE_DIM // 2)
            q_ref[hd, rows, LANES:2 * LANES] = (q_rot * scale).astype(BF16)
            k_ref[hd, rows, 0:LANES] = rk[:, hd * LANES:(hd + 1) * LANES].astype(BF16)
            k_ref[hd, rows, LANES:2 * LANES] = kr
            v_ref[hd, rows, :] = rv[:, hd * V_HEAD_DIM:(hd + 1) * V_HEAD_DIM].astype(BF16)


def _mla_up(cq, ckv, kr, pos, invfb, wq, wk, wv, *, batch, n_heads, tm):
    T = cq.shape[0]
    S = T // batch
    assert S % tm == 0
    tiles = S // tm
    const = lambda i: (0, 0)
    row = lambda i: (i, 0)
    head_major = lambda width: pl.BlockSpec((None, n_heads, tm, width),
                                            lambda i: (i // tiles, 0, i % tiles, 0))
    head_shape = lambda width: jax.ShapeDtypeStruct((batch, n_heads, S, width), BF16)
    return pl.pallas_call(
        functools.partial(_mla_up_kernel, n_heads=n_heads),
        grid=(T // tm,),
        in_specs=[
            pl.BlockSpec((tm, cq.shape[1]), row),
            pl.BlockSpec((tm, ckv.shape[1]), row),
            pl.BlockSpec((tm, LANES), row),
            pl.BlockSpec((tm, 1), row),
            pl.BlockSpec((1, LANES), const),
            pl.BlockSpec(wq.shape, const),
            pl.BlockSpec(wk.shape, const),
            pl.BlockSpec(wv.shape, const),
        ],
        out_specs=[head_major(MLA_QK_PAD), head_major(MLA_QK_PAD), head_major(V_HEAD_DIM)],
        out_shape=[head_shape(MLA_QK_PAD), head_shape(MLA_QK_PAD), head_shape(V_HEAD_DIM)],
        compiler_params=pltpu.CompilerParams(
            dimension_semantics=("parallel",), vmem_limit_bytes=VMEM_LIMIT),
        name="mla_up",
    )(cq, ckv, kr, pos, invfb, wq, wk, wv)


DSWA_GROUP = 16
MAX_DIL = 16
NT_DIMS = (((1,), (1,)), ((), ()))


def _aligned(x, multiple):
    return x if isinstance(x, int) else pl.multiple_of(x, multiple)


def _softmax_block(q, k, v, bias, prev):
    nk = k.shape[0]
    s = lax.dot_general(q, k, NT_DIMS, preferred_element_type=F32)
    blocks = [s[:, i * LANES:(i + 1) * LANES] for i in range(nk // LANES)]
    if bias is not None:
        nb = bias.shape[1] // LANES
        blocks[-nb:] = [b + bias[:, i * LANES:(i + 1) * LANES]
                        for i, b in enumerate(blocks[-nb:])]
    m_cur = jnp.max(functools.reduce(jnp.maximum, blocks), axis=1, keepdims=True)
    if prev is None:
        m_new = jnp.broadcast_to(m_cur, (q.shape[0], LANES))
    else:
        m_prev, l_prev, acc_prev = prev
        m_new = jnp.maximum(m_prev, m_cur)
        alpha = jnp.exp2(m_prev - m_new)
    p = jnp.concatenate([jnp.exp2(b - m_new) for b in blocks], axis=1).astype(BF16)
    v_aug = jnp.concatenate([v, jnp.ones((nk, LANES), BF16)], axis=1)
    pv = jnp.dot(p, v_aug, preferred_element_type=F32)
    acc_cur, l_cur = pv[:, :LANES], pv[:, LANES:]
    if prev is None:
        return m_new, l_cur, acc_cur
    return m_new, alpha * l_prev + l_cur, alpha * acc_prev + acc_cur


def _dswa_kernel(q_ref, k_ref, v_ref, o_ref, xq, xk, xv, mx, lx, ax, mn, ln, an, x4, *, seq):
    U = seq // MAX_DIL
    QB = Q_BLOCK
    NEG = NEG_INF

    def band_bias(dist):
        return jnp.where(dist >= 0, jnp.where(dist <= QB, 0.0, NEG), NEG).astype(F32)

    i1 = lax.broadcasted_iota(jnp.int32, (QB, QB), 0)
    j1 = lax.broadcasted_iota(jnp.int32, (QB, QB), 1)
    i2 = lax.broadcasted_iota(jnp.int32, (QB, 2 * QB), 0)
    j2 = lax.broadcasted_iota(jnp.int32, (QB, 2 * QB), 1)
    bias_first = band_bias(i1 - j1)
    bias_band = band_bias(i2 + QB - j2)
    qc = QB // 4
    lo = lambda x, n: x & (n - 1)
    hi = lambda x, n: x >> (n.bit_length() - 1)
    d4_first = band_bias(4 * (lo(i1, qc) - lo(j1, qc)) + hi(i1, qc) - hi(j1, qc))
    d4_band = band_bias(4 * (lo(i2, qc) - lo(j2, 2 * qc) + qc) + hi(i2, qc) - hi(j2, 2 * qc))

    for src, dst in ((q_ref, xq), (k_ref, xk), (v_ref, xv)):
        for r4 in range(4):
            x4[r4] = src[pl.ds(r4, 4 * U, stride=4), :]
        for r in range(MAX_DIL):
            dst[r] = x4[r % 4, pl.ds(r // 4, U, stride=4), :].astype(BF16)

    def d16_group(g, carry):
        out = []
        for j in range(DSWA_GROUP):
            r = g * DSWA_GROUP + j
            out.append((r, 0, _softmax_block(xq[r, 0:QB, :], xk[r, 0:QB, :], xv[r, 0:QB, :],
                                             bias_first, None)))
            for n in range(1, U // QB):
                ks = slice((n - 1) * QB, (n + 1) * QB)
                out.append((r, n, _softmax_block(xq[r, n * QB:(n + 1) * QB, :], xk[r, ks, :],
                                                 xv[r, ks, :], bias_band, None)))
        for r, n, (m_new, l_new, acc_new) in out:
            qs = slice(n * QB, (n + 1) * QB)
            mx[r, qs, :] = m_new
            lx[r, qs, :] = l_new
            ax[r, qs, :] = acc_new
        return carry
    lax.fori_loop(0, MAX_DIL // DSWA_GROUP, d16_group, 0)

    def d4_tile(r4, a, first):
        def gather(ref, start, size):
            return jnp.concatenate([ref[4 * qq + r4, pl.ds(start, size), :] for qq in range(4)],
                                   axis=0)
        q = gather(xq, a, qc)
        prev = (gather(mx, a, qc), gather(lx, a, qc), gather(ax, a, qc))
        if first:
            k, v, bias = gather(xk, a, qc), gather(xv, a, qc), d4_first
        else:
            k, v, bias = gather(xk, a - qc, 2 * qc), gather(xv, a - qc, 2 * qc), d4_band
        return _softmax_block(q, k, v, bias, prev)

    def d4_store(r4, a, res):
        for ref, val in zip((mx, lx, ax), res):
            for qq in range(4):
                ref[4 * qq + r4, pl.ds(a, qc), :] = val[qq * qc:(qq + 1) * qc]

    def d4_group(g, first_group):
        offs = [_aligned((g * (DSWA_GROUP // 4) + i) * qc, qc) for i in range(DSWA_GROUP // 4)]
        res = [(r4, a, d4_tile(r4, a, first_group and i == 0))
               for i, a in enumerate(offs) for r4 in range(4)]
        for r4, a, val in res:
            d4_store(r4, a, val)

    d4_group(0, True)
    lax.fori_loop(1, U // qc // (DSWA_GROUP // 4), lambda g, c: (d4_group(g, False), c)[1], 0)

    for src, dst in ((mx, mn), (lx, ln), (ax, an)):
        for r in range(MAX_DIL):
            x4[r % 4, pl.ds(r // 4, U, stride=4), :] = src[r]
        for r4 in range(4):
            dst[pl.ds(r4, 4 * U, stride=4), :] = x4[r4]

    def d1_tile(n, first):
        qs = pl.ds(_aligned(n * QB, QB), QB)
        if first:
            ks, bias = qs, bias_first
        else:
            ks, bias = pl.ds(_aligned((n - 1) * QB, QB), 2 * QB), bias_band
        prev = (mn[qs, :], ln[qs, :], an[qs, :])
        _, l_new, acc_new = _softmax_block(q_ref[qs, :].astype(BF16), k_ref[ks, :].astype(BF16),
                                           v_ref[ks, :].astype(BF16), bias, prev)
        return qs, (acc_new / l_new).astype(o_ref.dtype)

    def d1_group(g, first_group):
        res = [d1_tile(g * DSWA_GROUP + j, first_group and j == 0) for j in range(DSWA_GROUP)]
        for qs, val in res:
            o_ref[qs, :] = val

    d1_group(0, True)
    lax.fori_loop(1, seq // QB // DSWA_GROUP, lambda g, c: (d1_group(g, False), c)[1], 0)


def _dswa(aq, ak, av):
    B, n_heads, S, _ = aq.shape
    U = S // MAX_DIL
    assert DILATIONS == (1, 4, MAX_DIL) and MAX_DIL % DSWA_GROUP == 0
    assert U % Q_BLOCK == 0 and (S // Q_BLOCK) % DSWA_GROUP == 0
    spec = pl.BlockSpec((None, None, S, HEAD_DIM), lambda b, h: (b, h, 0, 0))
    x16 = lambda dt: pltpu.VMEM((MAX_DIL, U, LANES), dt)
    return pl.pallas_call(
        functools.partial(_dswa_kernel, seq=S),
        grid=(B, n_heads),
        in_specs=[spec, spec, spec],
        out_specs=pl.BlockSpec((None, S, HEAD_DIM), lambda b, h: (b, 0, h)),
        out_shape=jax.ShapeDtypeStruct((B, S, n_heads * HEAD_DIM), BF16),
        scratch_shapes=([x16(BF16)] * 3 + [x16(F32)] * 3 + [pltpu.VMEM((S, LANES), F32)] * 3
                        + [pltpu.VMEM((4, S // 4, LANES), F32)]),
        compiler_params=pltpu.CompilerParams(
            dimension_semantics=("parallel", "parallel"), vmem_limit_bytes=VMEM_LIMIT),
        name="dswa",
    )(aq, ak, av)


MLA_CHAIN = 256


def _cast_specs(weights, n_steps, index):
    specs, shapes = [], []
    for w in weights:
        rows = w.shape[0] // n_steps
        assert rows * n_steps == w.shape[0] and rows % 16 == 0
        specs.append(pl.BlockSpec((rows, w.shape[1]), index))
        shapes.append(jax.ShapeDtypeStruct(w.shape, BF16))
    return specs, shapes


def _mla_attn_kernel(q_ref, k_ref, v_ref, *rest, seq, tk):
    n_cast = (len(rest) - 1) // 2
    o_ref = rest[n_cast]
    for src, dst in zip(rest[:n_cast], rest[n_cast + 1:]):
        dst[...] = src[...].astype(BF16)
    _mla_attn_body(q_ref, k_ref, v_ref, o_ref, seq=seq, tk=tk)


def _mla_attn_body(q_ref, k_ref, v_ref, o_ref, *, seq, tk):
    cq = MLA_CHAIN
    n_chains = seq // cq
    ri = lax.broadcasted_iota(jnp.int32, (cq, cq), 0)
    ci = lax.broadcasted_iota(jnp.int32, (cq, cq), 1)
    diag_bias = jnp.where(ci <= ri, 0.0, NEG_INF).astype(F32)

    plans = []
    for c in range(n_chains):
        n_full = (c * cq) // tk
        plan = [(j * tk, tk, None) for j in range(n_full)]
        plan.append((n_full * tk, (c + 1) * cq - n_full * tk, diag_bias))
        plans.append(plan)

    state = [None] * n_chains
    for t in range(max(len(p) for p in plans)):
        for c in reversed(range(n_chains)):
            if t >= len(plans[c]):
                continue
            start, nk, bias = plans[c][t]
            rows = slice(c * cq, (c + 1) * cq)
            ks = slice(start, start + nk)
            state[c] = _softmax_block(q_ref[rows, :], k_ref[ks, :], v_ref[ks, :], bias, state[c])
            if t == len(plans[c]) - 1:
                _, l, acc = state[c]
                o_ref[rows, :] = (acc / l).astype(o_ref.dtype)


def _mla_attn(q, k, v, cast_weights, *, tk):
    B, n_heads, S, _ = q.shape
    assert tk % MLA_CHAIN == 0 and S % tk == 0
    head = lambda width: pl.BlockSpec((None, None, S, width), lambda b, h: (b, h, 0, 0))
    cast_specs, cast_shapes = _cast_specs(cast_weights, B * n_heads,
                                          lambda b, h: (b * n_heads + h, 0))
    return pl.pallas_call(
        functools.partial(_mla_attn_kernel, seq=S, tk=tk),
        grid=(B, n_heads),
        in_specs=[head(MLA_QK_PAD), head(MLA_QK_PAD), head(V_HEAD_DIM)] + cast_specs,
        out_specs=[pl.BlockSpec((None, S, V_HEAD_DIM), lambda b, h: (b, 0, h))] + cast_specs,
        out_shape=[jax.ShapeDtypeStruct((B, S, n_heads * V_HEAD_DIM), BF16)] + cast_shapes,
        compiler_params=pltpu.CompilerParams(
            dimension_semantics=("parallel", "parallel"), vmem_limit_bytes=VMEM_LIMIT),
        name="mla_attn",
    )(q, k, v, *cast_weights)


def _outproj_kernel(a_ref, b_ref, x_ref, w_ref, g_ref, o_ref, *, a_width):
    tm = x_ref.shape[0]
    for part in range(ROW_SPLIT):
        rows = slice(part * tm // ROW_SPLIT, (part + 1) * tm // ROW_SPLIT)
        y = jnp.dot(a_ref[rows, :], w_ref[0:a_width, :], preferred_element_type=F32)
        y = y + jnp.dot(b_ref[rows, :], w_ref[a_width:, :], preferred_element_type=F32)
        o_ref[rows, :] = x_ref[rows, :] + _rms(y, g_ref[...])


def _outproj(a, b, x2d, w_out, g_post, *, tm):
    T, D = x2d.shape
    const = lambda i: (0, 0)
    row = lambda i: (i, 0)
    return pl.pallas_call(
        functools.partial(_outproj_kernel, a_width=a.shape[1]),
        grid=(T // tm,),
        in_specs=[
            pl.BlockSpec((tm, a.shape[1]), row),
            pl.BlockSpec((tm, b.shape[1]), row),
            pl.BlockSpec((tm, D), row),
            pl.BlockSpec(w_out.shape, const, pipeline_mode=pl.Buffered(1)),
            pl.BlockSpec((1, D), const),
        ],
        out_specs=pl.BlockSpec((tm, D), row),
        out_shape=jax.ShapeDtypeStruct((T, D), F32),
        compiler_params=pltpu.CompilerParams(
            dimension_semantics=("parallel",), vmem_limit_bytes=VMEM_LIMIT),
        name="outproj",
    )(a, b, x2d, w_out, g_post)


def _mlp_kernel(x_ref, gpre_ref, gpost_ref, wu_ref, wd_ref, o_ref, h_sc):
    f = pl.program_id(1)
    last = pl.num_programs(1) - 1
    tm = x_ref.shape[0]

    def step(first, final):
        for part in range(ROW_SPLIT):
            rows = slice(part * tm // ROW_SPLIT, (part + 1) * tm // ROW_SPLIT)
            if first:
                h = _rms(x_ref[rows, :], gpre_ref[...]).astype(BF16)
                h_sc[rows, :] = h
            else:
                h = h_sc[rows, :]
            u = jnp.dot(h, wu_ref[...], preferred_element_type=F32)
            u = jnp.square(jnp.maximum(u, 0.0)).astype(BF16)
            y = jnp.dot(u, wd_ref[...], preferred_element_type=F32)
            acc = y if first else o_ref[rows, :] + y
            if final:
                o_ref[rows, :] = x_ref[rows, :] + _rms(acc, gpost_ref[...])
            else:
                o_ref[rows, :] = acc

    pl.when(f == 0)(lambda: step(True, False))
    pl.when(jnp.logical_and(f > 0, f < last))(lambda: step(False, False))
    pl.when(f == last)(lambda: step(False, True))


def _mlp(x2d, g_pre, g_post, w_up, w_down, *, tm, tf):
    T, D = x2d.shape
    F = w_up.shape[1]
    return pl.pallas_call(
        _mlp_kernel,
        grid=(T // tm, F // tf),
        in_specs=[
            pl.BlockSpec((tm, D), lambda i, f: (i, 0)),
            pl.BlockSpec((1, D), lambda i, f: (0, 0)),
            pl.BlockSpec((1, D), lambda i, f: (0, 0)),
            pl.BlockSpec((D, tf), lambda i, f: (0, f)),
            pl.BlockSpec((tf, D), lambda i, f: (f, 0)),
        ],
        out_specs=pl.BlockSpec((tm, D), lambda i, f: (i, 0), pipeline_mode=pl.Buffered(1)),
        out_shape=jax.ShapeDtypeStruct((T, D), F32),
        scratch_shapes=[pltpu.VMEM((tm, D), BF16)],
        compiler_params=pltpu.CompilerParams(
            dimension_semantics=("parallel", "arbitrary"), vmem_limit_bytes=VMEM_LIMIT),
        name="mlp",
    )(x2d, g_pre, g_post, w_up, w_down)


def _cast_pad_kernel(w_ref, o_ref, *, aligned):
    o_ref[:, 0:aligned] = w_ref[:, 0:aligned].astype(BF16)
    tail = w_ref[:, aligned:].astype(BF16)
    o_ref[:, aligned:] = jnp.concatenate(
        [tail, jnp.zeros((tail.shape[0], o_ref.shape[1] - w_ref.shape[1]), BF16)], axis=1)


def _cast_pad_cols(w, extra_cols, *, rows_per_step):
    rows, cols = w.shape
    aligned = cols // LANES * LANES
    assert rows % rows_per_step == 0 and (cols + extra_cols) % LANES == 0 and aligned < cols
    return pl.pallas_call(
        functools.partial(_cast_pad_kernel, aligned=aligned),
        grid=(rows // rows_per_step,),
        in_specs=[pl.BlockSpec((rows_per_step, cols), lambda i: (i, 0))],
        out_specs=pl.BlockSpec((rows_per_step, cols + extra_cols), lambda i: (i, 0)),
        out_shape=jax.ShapeDtypeStruct((rows, cols + extra_cols), BF16),
        compiler_params=pltpu.CompilerParams(
            dimension_semantics=("parallel",), vmem_limit_bytes=VMEM_LIMIT),
        name="cast_pad",
    )(w)


def _inv_freq_lanes(rot_dim):
    inv = ROPE_THETA ** (-jnp.arange(0, rot_dim, 2, dtype=F32) / rot_dim)
    lanes = jnp.concatenate([inv, inv, jnp.zeros((LANES - rot_dim,), F32)])
    return lanes.reshape(1, LANES)


def _layer(x2d, pos, B, S, norm_attn_pre, norm_attn_post, w_in, q_latent_norm, kv_latent_norm,
           w_uq, w_ukv, w_out, norm_mlp_pre, norm_mlp_post, w_up, w_down):
    D = x2d.shape[1]
    q_rank = q_latent_norm.shape[0]
    kv_rank = kv_latent_norm.shape[0]
    n_mla = w_uq.shape[1] // (QK_NOPE_DIM + QK_ROPE_DIM)
    a_width = (w_in.shape[1] - q_rank - kv_rank - QK_ROPE_DIM) // 3

    w_in_p = _cast_pad_cols(w_in, LANES - QK_ROPE_DIM, rows_per_step=256)
    wq3 = w_uq.reshape(q_rank, n_mla, QK_NOPE_DIM + QK_ROPE_DIM)
    wq_p = jnp.pad(wq3, ((0, 0), (0, 0), (0, MLA_QK_PAD - QK_NOPE_DIM - QK_ROPE_DIM)))
    wq_p = wq_p.reshape(q_rank, n_mla * MLA_QK_PAD).astype(BF16)
    wkv3 = w_ukv.reshape(kv_rank, n_mla, QK_NOPE_DIM + V_HEAD_DIM)
    wk_p = wkv3[:, :, :QK_NOPE_DIM].reshape(kv_rank, n_mla * QK_NOPE_DIM).astype(BF16)
    wv_p = wkv3[:, :, QK_NOPE_DIM:].reshape(kv_rank, n_mla * V_HEAD_DIM).astype(BF16)

    invfa = _inv_freq_lanes(ROT_DIM)
    invfb = _inv_freq_lanes(QK_ROPE_DIM)
    row = lambda g: g.reshape(1, -1)

    aq, ak, av, cq, ckv, kr = _inproj(
        x2d, pos, row(norm_attn_pre), w_in_p, invfa, invfb, row(q_latent_norm),
        row(kv_latent_norm), batch=B, a_width=a_width, q_rank=q_rank, kv_rank=kv_rank, tm=512)
    q_b, k_b, v_b = _mla_up(cq, ckv, kr, pos, invfb, wq_p, wk_p, wv_p, batch=B, n_heads=n_mla,
                            tm=512)

    a_out = _dswa(aq, ak, av)
    b_out, w_out_b, w_up_b, w_down_b = _mla_attn(q_b, k_b, v_b, [w_out, w_up, w_down], tk=512)

    x2d = _outproj(a_out.reshape(B * S, -1), b_out.reshape(B * S, -1), x2d,
                   w_out_b, row(norm_attn_post), tm=512)
    return _mlp(x2d, row(norm_mlp_pre), row(norm_mlp_post), w_up_b, w_down_b, tm=1024, tf=1024)


def kernel(x, positions, norm_attn_pre, norm_attn_post, w_in, q_latent_norm, kv_latent_norm,
           w_uq, w_ukv, w_out, norm_mlp_pre, norm_mlp_post, w_up, w_down):
    B, S, D = x.shape
    x2d = x.reshape(B * S, D)
    pos = positions.astype(F32).reshape(B * S, 1)
    for layer in range(w_in.shape[0]):
        x2d = _layer(x2d, pos, B, S, norm_attn_pre[layer], norm_attn_post[layer], w_in[layer],
                     q_latent_norm[layer], kv_latent_norm[layer], w_uq[layer], w_ukv[layer],
                     w_out[layer], norm_mlp_pre[layer], norm_mlp_post[layer], w_up[layer],
                     w_down[layer])
    return x2d.reshape(B, S, D)
```

```python
import functools

import jax
import jax.numpy as jnp
from jax import lax
from jax.experimental import pallas as pl
from jax.experimental.pallas import tpu as pltpu

F32 = jnp.float32
BF16 = jnp.bfloat16

LANES = 128
HEAD_DIM = 128
ROT_DIM = HEAD_DIM // 4
ROPE_THETA = 500000.0
QK_NOPE_DIM = 128
QK_ROPE_DIM = 64
V_HEAD_DIM = 128
MLA_QK_PAD = 256
Q_BLOCK = 128
DILATIONS = (1, 4, 16)
NORM_EPS = 1e-6
NEG_INF = -1e30
LOG2E = 1.4426950408889634
VMEM_LIMIT = 56 * 1024 * 1024


def _rms(x, gain):
    ms = jnp.mean(x * x, axis=-1, keepdims=True)
    return x * lax.rsqrt(ms + NORM_EPS) * gain


def _rope_tables(pos, inv_freq, half):
    lane = lax.broadcasted_iota(jnp.int32, (1, LANES), 1)
    ang = pos * inv_freq
    sign = jnp.where(lane < half, -1.0, 1.0).astype(F32)
    return jnp.cos(ang), jnp.sin(ang) * sign


def _rope(r, cos, sin, half):
    lane = lax.broadcasted_iota(jnp.int32, (1, LANES), 1)
    partner = jnp.where(lane < half,
                        pltpu.roll(r, LANES - half, 1),
                        pltpu.roll(r, half, 1))
    return r * cos + partner * sin


ROW_SPLIT = 2


def _inproj_kernel(x_ref, pos_ref, g_ref, w_ref, invfa_ref, invfb_ref, gq_ref, gkv_ref,
                   aq_ref, ak_ref, av_ref, cq_ref, ckv_ref, kr_ref, cosb_ref, sinb_ref, *,
                   a_width, q_rank, kv_rank):
    n_heads = a_width // HEAD_DIM
    q_scale = HEAD_DIM ** -0.5 * LOG2E
    tm = x_ref.shape[0]
    for part in range(ROW_SPLIT):
        rows = slice(part * tm // ROW_SPLIT, (part + 1) * tm // ROW_SPLIT)
        h = _rms(x_ref[rows, :], g_ref[...]).astype(BF16)
        pos = pos_ref[rows, :]
        cos_a, sin_a = _rope_tables(pos, invfa_ref[...], ROT_DIM // 2)
        cos_b, sin_b = _rope_tables(pos, invfb_ref[...], QK_ROPE_DIM // 2)
        cosb_ref[rows, :] = cos_b
        sinb_ref[rows, :] = sin_b

        r = jnp.dot(h, w_ref[:, 0:a_width], preferred_element_type=F32)
        for hd in range(n_heads):
            sl = slice(hd * HEAD_DIM, (hd + 1) * HEAD_DIM)
            aq_ref[hd, rows, :] = _rope(r[:, sl], cos_a, sin_a, ROT_DIM // 2) * q_scale
        r = jnp.dot(h, w_ref[:, a_width:2 * a_width], preferred_element_type=F32)
        for hd in range(n_heads):
            sl = slice(hd * HEAD_DIM, (hd + 1) * HEAD_DIM)
            ak_ref[hd, rows, :] = _rope(r[:, sl], cos_a, sin_a, ROT_DIM // 2)
        r = jnp.dot(h, w_ref[:, 2 * a_width:3 * a_width], preferred_element_type=F32)
        for hd in range(n_heads):
            av_ref[hd, rows, :] = r[:, hd * HEAD_DIM:(hd + 1) * HEAD_DIM]

        c0 = 3 * a_width
        c1 = c0 + q_rank + kv_rank
        r = jnp.dot(h, w_ref[:, c0:c1], preferred_element_type=F32)
        cq_ref[rows, :] = _rms(r[:, 0:q_rank], gq_ref[...]).astype(BF16)
        ckv_ref[rows, :] = _rms(r[:, q_rank:q_rank + kv_rank], gkv_ref[...]).astype(BF16)
        kr = jnp.dot(h, w_ref[:, c1:], preferred_element_type=F32)
        kr = jnp.concatenate([kr, jnp.zeros((kr.shape[0], LANES - kr.shape[1]), F32)], axis=1)
        kr_ref[rows, :] = _rope(kr, cos_b, sin_b, QK_ROPE_DIM // 2).astype(BF16)


def _inproj(x2d, pos, g_pre, w_in, invfa, invfb, gq, gkv, *, batch, a_width, q_rank, kv_rank, tm):
    T, D = x2d.shape
    S = T // batch
    assert S % tm == 0
    tiles = S // tm
    n_heads = a_width // HEAD_DIM
    ncols = w_in.shape[1]
    const = lambda i: (0, 0)
    row = lambda i: (i, 0)
    head_major = pl.BlockSpec((None, n_heads, tm, HEAD_DIM),
                              lambda i: (i // tiles, 0, i % tiles, 0))
    head_shape = jax.ShapeDtypeStruct((batch, n_heads, S, HEAD_DIM), F32)
    kern = functools.partial(_inproj_kernel, a_width=a_width, q_rank=q_rank, kv_rank=kv_rank)
    return pl.pallas_call(
        kern,
        grid=(T // tm,),
        in_specs=[
            pl.BlockSpec((tm, D), row),
            pl.BlockSpec((tm, 1), row),
            pl.BlockSpec((1, D), const),
            pl.BlockSpec((D, ncols), const, pipeline_mode=pl.Buffered(1)),
            pl.BlockSpec((1, LANES), const),
            pl.BlockSpec((1, LANES), const),
            pl.BlockSpec((1, q_rank), const),
            pl.BlockSpec((1, kv_rank), const),
        ],
        out_specs=[
            head_major, head_major, head_major,
            pl.BlockSpec((tm, q_rank), row),
            pl.BlockSpec((tm, kv_rank), row),
            pl.BlockSpec((tm, LANES), row),
            pl.BlockSpec((tm, LANES), row),
            pl.BlockSpec((tm, LANES), row),
        ],
        out_shape=[
            head_shape, head_shape, head_shape,
            jax.ShapeDtypeStruct((T, q_rank), BF16),
            jax.ShapeDtypeStruct((T, kv_rank), BF16),
            jax.ShapeDtypeStruct((T, LANES), BF16),
            jax.ShapeDtypeStruct((T, LANES), F32),
            jax.ShapeDtypeStruct((T, LANES), F32),
        ],
        compiler_params=pltpu.CompilerParams(
            dimension_semantics=("parallel",), vmem_limit_bytes=VMEM_LIMIT),
        name="inproj",
    )(x2d, pos, g_pre, w_in, invfa, invfb, gq, gkv)


def _mla_up_kernel(cq_ref, ckv_ref, kr_ref, cosb_ref, sinb_ref, wq_ref, wk_ref, wv_ref,
                   q_ref, k_ref, v_ref, *, n_heads):
    scale = (QK_NOPE_DIM + QK_ROPE_DIM) ** -0.5 * LOG2E
    tm = cq_ref.shape[0]
    for part in range(ROW_SPLIT):
        rows = slice(part * tm // ROW_SPLIT, (part + 1) * tm // ROW_SPLIT)
        cos_b, sin_b = cosb_ref[rows, :], sinb_ref[rows, :]
        rq = jnp.dot(cq_ref[rows, :], wq_ref[...], preferred_element_type=F32)
        ckv = ckv_ref[rows, :]
        rk = jnp.dot(ckv, wk_ref[...], preferred_element_type=F32)
        kr = kr_ref[rows, :]
        rv = jnp.dot(ckv, wv_ref[...], preferred_element_type=F32)
        for hd in range(n_heads):
            base = hd * MLA_QK_PAD
            q_ref[hd, rows, 0:LANES] = (rq[:, base:base + LANES] * scale).astype(BF16)
            q_rot = _rope(rq[:, base + LANES:base + 2 * LANES], cos_b, sin_b, QK_ROPE_DIM // 2)
            q_ref[hd, rows, LANES:2 * LANES] = (q_rot * scale).astype(BF16)
            k_ref[hd, rows, 0:LANES] = rk[:, hd * LANES:(hd + 1) * LANES].astype(BF16)
            k_ref[hd, rows, LANES:2 * LANES] = kr
            v_ref[hd, rows, :] = rv[:, hd * V_HEAD_DIM:(hd + 1) * V_HEAD_DIM].astype(BF16)


def _mla_up(cq, ckv, kr, cos_b, sin_b, wq, wk, wv, *, batch, n_heads, tm):
    T = cq.shape[0]
    S = T // batch
    assert S % tm == 0
    tiles = S // tm
    const = lambda i: (0, 0)
    row = lambda i: (i, 0)
    head_major = lambda width: pl.BlockSpec((None, n_heads, tm, width),
                                            lambda i: (i // tiles, 0, i % tiles, 0))
    head_shape = lambda width: jax.ShapeDtypeStruct((batch, n_heads, S, width), BF16)
    return pl.pallas_call(
        functools.partial(_mla_up_kernel, n_heads=n_heads),
        grid=(T // tm,),
        in_specs=[
            pl.BlockSpec((tm, cq.shape[1]), row),
            pl.BlockSpec((tm, ckv.shape[1]), row),
            pl.BlockSpec((tm, LANES), row),
            pl.BlockSpec((tm, LANES), row),
            pl.BlockSpec((tm, LANES), row),
            pl.BlockSpec(wq.shape, const),
            pl.BlockSpec(wk.shape, const),
            pl.BlockSpec(wv.shape, const),
        ],
        out_specs=[head_major(MLA_QK_PAD), head_major(MLA_QK_PAD), head_major(V_HEAD_DIM)],
        out_shape=[head_shape(MLA_QK_PAD), head_shape(MLA_QK_PAD), head_shape(V_HEAD_DIM)],
        compiler_params=pltpu.CompilerParams(
            dimension_semantics=("parallel",), vmem_limit_bytes=VMEM_LIMIT),
        name="mla_up",
    )(cq, ckv, kr, cos_b, sin_b, wq, wk, wv)


MAX_DIL = 16
NT_DIMS = (((1,), (1,)), ((), ()))


def _softmax_block(q, k, v, bias, prev):
    nk = k.shape[0]
    s = lax.dot_general(q, k, NT_DIMS, preferred_element_type=F32)
    blocks = [s[:, i * LANES:(i + 1) * LANES] for i in range(nk // LANES)]
    if bias is not None:
        nb = bias.shape[1] // LANES
        blocks[-nb:] = [b + bias[:, i * LANES:(i + 1) * LANES]
                        for i, b in enumerate(blocks[-nb:])]
    m_cur = jnp.max(functools.reduce(jnp.maximum, blocks), axis=1, keepdims=True)
    if prev is None:
        m_new = jnp.broadcast_to(m_cur, (q.shape[0], LANES))
    else:
        m_prev, l_prev, acc_prev = prev
        m_new = jnp.maximum(m_prev, m_cur)
        alpha = jnp.exp2(m_prev - m_new)
    p = jnp.concatenate([jnp.exp2(b - m_new) for b in blocks], axis=1).astype(BF16)
    v_aug = jnp.concatenate([v, jnp.ones((nk, LANES), BF16)], axis=1)
    pv = jnp.dot(p, v_aug, preferred_element_type=F32)
    acc_cur, l_cur = pv[:, :LANES], pv[:, LANES:]
    if prev is None:
        return m_new, l_cur, acc_cur
    return m_new, alpha * l_prev + l_cur, alpha * acc_prev + acc_cur


def _dswa_kernel(q_ref, k_ref, v_ref, o_ref, xq, xk, xv, mx, lx, ax, mn, ln, an, x4, *, seq):
    U = seq // MAX_DIL
    QB = Q_BLOCK
    NEG = NEG_INF

    def band_bias(dist):
        return jnp.where(dist >= 0, jnp.where(dist <= QB, 0.0, NEG), NEG).astype(F32)

    i1 = lax.broadcasted_iota(jnp.int32, (QB, QB), 0)
    j1 = lax.broadcasted_iota(jnp.int32, (QB, QB), 1)
    i2 = lax.broadcasted_iota(jnp.int32, (QB, 2 * QB), 0)
    j2 = lax.broadcasted_iota(jnp.int32, (QB, 2 * QB), 1)
    bias_first = band_bias(i1 - j1)
    bias_band = band_bias(i2 + QB - j2)
    qc = QB // 4
    lo = lambda x, n: x & (n - 1)
    hi = lambda x, n: x >> (n.bit_length() - 1)
    d4_first = band_bias(4 * (lo(i1, qc) - lo(j1, qc)) + hi(i1, qc) - hi(j1, qc))
    d4_band = band_bias(4 * (lo(i2, qc) - lo(j2, 2 * qc) + qc) + hi(i2, qc) - hi(j2, 2 * qc))

    for src, dst in ((q_ref, xq), (k_ref, xk), (v_ref, xv)):
        for r4 in range(4):
            x4[r4] = src[pl.ds(r4, 4 * U, stride=4), :]
        for r in range(MAX_DIL):
            dst[r] = x4[r % 4, pl.ds(r // 4, U, stride=4), :].astype(BF16)

    for r in range(MAX_DIL):
        for n in range(U // QB):
            qs = slice(n * QB, (n + 1) * QB)
            ks = qs if n == 0 else slice((n - 1) * QB, (n + 1) * QB)
            res = _softmax_block(xq[r, qs, :], xk[r, ks, :], xv[r, ks, :],
                                 bias_first if n == 0 else bias_band, None)
            for ref, val in zip((mx, lx, ax), res):
                ref[r, qs, :] = val

    def d4_tile(r4, a, first):
        def gather(ref, start, size):
            return jnp.concatenate([ref[4 * qq + r4, start:start + size, :] for qq in range(4)],
                                   axis=0)
        q = gather(xq, a, qc)
        prev = (gather(mx, a, qc), gather(lx, a, qc), gather(ax, a, qc))
        if first:
            k, v, bias = gather(xk, a, qc), gather(xv, a, qc), d4_first
        else:
            k, v, bias = gather(xk, a - qc, 2 * qc), gather(xv, a - qc, 2 * qc), d4_band
        return _softmax_block(q, k, v, bias, prev)

    for a in range(0, U, qc):
        for r4 in range(4):
            res = d4_tile(r4, a, a == 0)
            for ref, val in zip((mx, lx, ax), res):
                for qq in range(4):
                    ref[4 * qq + r4, a:a + qc, :] = val[qq * qc:(qq + 1) * qc]

    for src, dst in ((mx, mn), (lx, ln), (ax, an)):
        for r in range(MAX_DIL):
            x4[r % 4, pl.ds(r // 4, U, stride=4), :] = src[r]
        for r4 in range(4):
            dst[pl.ds(r4, 4 * U, stride=4), :] = x4[r4]

    for n in range(seq // QB):
        qs = slice(n * QB, (n + 1) * QB)
        ks = qs if n == 0 else slice((n - 1) * QB, (n + 1) * QB)
        prev = (mn[qs, :], ln[qs, :], an[qs, :])
        _, l_new, acc_new = _softmax_block(
            q_ref[qs, :].astype(BF16), k_ref[ks, :].astype(BF16), v_ref[ks, :].astype(BF16),
            bias_first if n == 0 else bias_band, prev)
        o_ref[qs, :] = (acc_new / l_new).astype(o_ref.dtype)


def _dswa(aq, ak, av):
    B, n_heads, S, _ = aq.shape
    U = S // MAX_DIL
    assert DILATIONS == (1, 4, MAX_DIL) and U % Q_BLOCK == 0
    spec = pl.BlockSpec((None, None, S, HEAD_DIM), lambda b, h: (b, h, 0, 0))
    x16 = lambda dt: pltpu.VMEM((MAX_DIL, U, LANES), dt)
    return pl.pallas_call(
        functools.partial(_dswa_kernel, seq=S),
        grid=(B, n_heads),
        in_specs=[spec, spec, spec],
        out_specs=pl.BlockSpec((None, S, HEAD_DIM), lambda b, h: (b, 0, h)),
        out_shape=jax.ShapeDtypeStruct((B, S, n_heads * HEAD_DIM), BF16),
        scratch_shapes=([x16(BF16)] * 3 + [x16(F32)] * 3 + [pltpu.VMEM((S, LANES), F32)] * 3
                        + [pltpu.VMEM((4, S // 4, LANES), F32)]),
        compiler_params=pltpu.CompilerParams(
            dimension_semantics=("parallel", "parallel"), vmem_limit_bytes=VMEM_LIMIT),
        name="dswa",
    )(aq, ak, av)


MLA_CHAIN = 256


def _cast_specs(weights, n_steps, index):
    specs, shapes = [], []
    for w in weights:
        rows = w.shape[0] // n_steps
        assert rows * n_steps == w.shape[0] and rows % 16 == 0
        specs.append(pl.BlockSpec((rows, w.shape[1]), index))
        shapes.append(jax.ShapeDtypeStruct(w.shape, BF16))
    return specs, shapes


def _mla_attn_kernel(q_ref, k_ref, v_ref, *rest, seq, tk):
    n_cast = (len(rest) - 1) // 2
    o_ref = rest[n_cast]
    for src, dst in zip(rest[:n_cast], rest[n_cast + 1:]):
        dst[...] = src[...].astype(BF16)
    _mla_attn_body(q_ref, k_ref, v_ref, o_ref, seq=seq, tk=tk)


def _mla_attn_body(q_ref, k_ref, v_ref, o_ref, *, seq, tk):
    cq = MLA_CHAIN
    n_chains = seq // cq
    ri = lax.broadcasted_iota(jnp.int32, (cq, cq), 0)
    ci = lax.broadcasted_iota(jnp.int32, (cq, cq), 1)
    diag_bias = jnp.where(ci <= ri, 0.0, NEG_INF).astype(F32)

    plans = []
    for c in range(n_chains):
        n_full = (c * cq) // tk
        plan = [(j * tk, tk, None) for j in range(n_full)]
        plan.append((n_full * tk, (c + 1) * cq - n_full * tk, diag_bias))
        plans.append(plan)

    state = [None] * n_chains
    for t in range(max(len(p) for p in plans)):
        for c in reversed(range(n_chains)):
            if t >= len(plans[c]):
                continue
            start, nk, bias = plans[c][t]
            rows = slice(c * cq, (c + 1) * cq)
            ks = slice(start, start + nk)
            state[c] = _softmax_block(q_ref[rows, :], k_ref[ks, :], v_ref[ks, :], bias, state[c])
            if t == len(plans[c]) - 1:
                _, l, acc = state[c]
                o_ref[rows, :] = (acc / l).astype(o_ref.dtype)


def _mla_attn(q, k, v, cast_weights, *, tk):
    B, n_heads, S, _ = q.shape
    assert tk % MLA_CHAIN == 0 and S % tk == 0
    head = lambda width: pl.BlockSpec((None, None, S, width), lambda b, h: (b, h, 0, 0))
    cast_specs, cast_shapes = _cast_specs(cast_weights, B * n_heads,
                                          lambda b, h: (b * n_heads + h, 0))
    return pl.pallas_call(
        functools.partial(_mla_attn_kernel, seq=S, tk=tk),
        grid=(B, n_heads),
        in_specs=[head(MLA_QK_PAD), head(MLA_QK_PAD), head(V_HEAD_DIM)] + cast_specs,
        out_specs=[pl.BlockSpec((None, S, V_HEAD_DIM), lambda b, h: (b, 0, h))] + cast_specs,
        out_shape=[jax.ShapeDtypeStruct((B, S, n_heads * V_HEAD_DIM), BF16)] + cast_shapes,
        compiler_params=pltpu.CompilerParams(
            dimension_semantics=("parallel", "parallel"), vmem_limit_bytes=VMEM_LIMIT),
        name="mla_attn",
    )(q, k, v, *cast_weights)


def _outproj_kernel(a_ref, b_ref, x_ref, w_ref, g_ref, o_ref, *, a_width):
    tm = x_ref.shape[0]
    for part in range(ROW_SPLIT):
        rows = slice(part * tm // ROW_SPLIT, (part + 1) * tm // ROW_SPLIT)
        y = jnp.dot(a_ref[rows, :], w_ref[0:a_width, :], preferred_element_type=F32)
        y = y + jnp.dot(b_ref[rows, :], w_ref[a_width:, :], preferred_element_type=F32)
        o_ref[rows, :] = x_ref[rows, :] + _rms(y, g_ref[...])


def _outproj(a, b, x2d, w_out, g_post, *, tm):
    T, D = x2d.shape
    const = lambda i: (0, 0)
    row = lambda i: (i, 0)
    return pl.pallas_call(
        functools.partial(_outproj_kernel, a_width=a.shape[1]),
        grid=(T // tm,),
        in_specs=[
            pl.BlockSpec((tm, a.shape[1]), row),
            pl.BlockSpec((tm, b.shape[1]), row),
            pl.BlockSpec((tm, D), row),
            pl.BlockSpec(w_out.shape, const, pipeline_mode=pl.Buffered(1)),
            pl.BlockSpec((1, D), const),
        ],
        out_specs=pl.BlockSpec((tm, D), row),
        out_shape=jax.ShapeDtypeStruct((T, D), F32),
        compiler_params=pltpu.CompilerParams(
            dimension_semantics=("parallel",), vmem_limit_bytes=VMEM_LIMIT),
        name="outproj",
    )(a, b, x2d, w_out, g_post)


def _mlp_kernel(x_ref, gpre_ref, gpost_ref, wu_ref, wd_ref, o_ref, h_sc):
    f = pl.program_id(1)
    last = pl.num_programs(1) - 1
    tm = x_ref.shape[0]

    def step(first, final):
        for part in range(ROW_SPLIT):
            rows = slice(part * tm // ROW_SPLIT, (part + 1) * tm // ROW_SPLIT)
            if first:
                h = _rms(x_ref[rows, :], gpre_ref[...]).astype(BF16)
                h_sc[rows, :] = h
            else:
                h = h_sc[rows, :]
            u = jnp.dot(h, wu_ref[...], preferred_element_type=F32)
            u = jnp.square(jnp.maximum(u, 0.0)).astype(BF16)
            y = jnp.dot(u, wd_ref[...], preferred_element_type=F32)
            acc = y if first else o_ref[rows, :] + y
            if final:
                o_ref[rows, :] = x_ref[rows, :] + _rms(acc, gpost_ref[...])
            else:
                o_ref[rows, :] = acc

    pl.when(f == 0)(lambda: step(True, False))
    pl.when(jnp.logical_and(f > 0, f < last))(lambda: step(False, False))
    pl.when(f == last)(lambda: step(False, True))


def _mlp(x2d, g_pre, g_post, w_up, w_down, *, tm, tf):
    T, D = x2d.shape
    F = w_up.shape[1]
    return pl.pallas_call(
        _mlp_kernel,
        grid=(T // tm, F // tf),
        in_specs=[
            pl.BlockSpec((tm, D), lambda i, f: (i, 0)),
            pl.BlockSpec((1, D), lambda i, f: (0, 0)),
            pl.BlockSpec((1, D), lambda i, f: (0, 0)),
            pl.BlockSpec((D, tf), lambda i, f: (0, f)),
            pl.BlockSpec((tf, D), lambda i, f: (f, 0)),
        ],
        out_specs=pl.BlockSpec((tm, D), lambda i, f: (i, 0)),
        out_shape=jax.ShapeDtypeStruct((T, D), F32),
        scratch_shapes=[pltpu.VMEM((tm, D), BF16)],
        compiler_params=pltpu.CompilerParams(
            dimension_semantics=("parallel", "arbitrary"), vmem_limit_bytes=VMEM_LIMIT),
        name="mlp",
    )(x2d, g_pre, g_post, w_up, w_down)


def _inv_freq_lanes(rot_dim):
    inv = ROPE_THETA ** (-jnp.arange(0, rot_dim, 2, dtype=F32) / rot_dim)
    lanes = jnp.concatenate([inv, inv, jnp.zeros((LANES - rot_dim,), F32)])
    return lanes.reshape(1, LANES)


def _layer(x2d, pos, B, S, norm_attn_pre, norm_attn_post, w_in, q_latent_norm, kv_latent_norm,
           w_uq, w_ukv, w_out, norm_mlp_pre, norm_mlp_post, w_up, w_down):
    D = x2d.shape[1]
    q_rank = q_latent_norm.shape[0]
    kv_rank = kv_latent_norm.shape[0]
    n_mla = w_uq.shape[1] // (QK_NOPE_DIM + QK_ROPE_DIM)
    a_width = (w_in.shape[1] - q_rank - kv_rank - QK_ROPE_DIM) // 3

    w_in_p = w_in.astype(BF16)
    wq3 = w_uq.reshape(q_rank, n_mla, QK_NOPE_DIM + QK_ROPE_DIM)
    wq_p = jnp.pad(wq3, ((0, 0), (0, 0), (0, MLA_QK_PAD - QK_NOPE_DIM - QK_ROPE_DIM)))
    wq_p = wq_p.reshape(q_rank, n_mla * MLA_QK_PAD).astype(BF16)
    wkv3 = w_ukv.reshape(kv_rank, n_mla, QK_NOPE_DIM + V_HEAD_DIM)
    wk_p = wkv3[:, :, :QK_NOPE_DIM].reshape(kv_rank, n_mla * QK_NOPE_DIM).astype(BF16)
    wv_p = wkv3[:, :, QK_NOPE_DIM:].reshape(kv_rank, n_mla * V_HEAD_DIM).astype(BF16)

    invfa = _inv_freq_lanes(ROT_DIM)
    invfb = _inv_freq_lanes(QK_ROPE_DIM)
    row = lambda g: g.reshape(1, -1)

    aq, ak, av, cq, ckv, kr, cos_b, sin_b = _inproj(
        x2d, pos, row(norm_attn_pre), w_in_p, invfa, invfb, row(q_latent_norm),
        row(kv_latent_norm), batch=B, a_width=a_width, q_rank=q_rank, kv_rank=kv_rank, tm=512)
    q_b, k_b, v_b = _mla_up(cq, ckv, kr, cos_b, sin_b, wq_p, wk_p, wv_p, batch=B, n_heads=n_mla,
                            tm=512)

    a_out = _dswa(aq, ak, av)
    b_out, w_out_b, w_up_b, w_down_b = _mla_attn(q_b, k_b, v_b, [w_out, w_up, w_down], tk=512)

    x2d = _outproj(a_out.reshape(B * S, -1), b_out.reshape(B * S, -1), x2d,
                   w_out_b, row(norm_attn_post), tm=512)
    return _mlp(x2d, row(norm_mlp_pre), row(norm_mlp_post), w_up_b, w_down_b, tm=512, tf=2048)


def kernel(x, positions, norm_attn_pre, norm_attn_post, w_in, q_latent_norm, kv_latent_norm,
           w_uq, w_ukv, w_out, norm_mlp_pre, norm_mlp_post, w_up, w_down):
    B, S, D = x.shape
    x2d = x.reshape(B * S, D)
    pos = positions.astype(F32).reshape(B * S, 1)
    for layer in range(w_in.shape[0]):
        x2d = _layer(x2d, pos, B, S, norm_attn_pre[layer], norm_attn_post[layer], w_in[layer],
                     q_latent_norm[layer], kv_latent_norm[layer], w_uq[layer], w_ukv[layer],
                     w_out[layer], norm_mlp_pre[layer], norm_mlp_post[layer], w_up[layer],
                     w_down[layer])
    return x2d.reshape(B, S, D)
```

```python
import functools

import jax
import jax.numpy as jnp
from jax import lax
from jax.experimental import pallas as pl
from jax.experimental.pallas import tpu as pltpu

F32 = jnp.float32
BF16 = jnp.bfloat16

LANES = 128
HEAD_DIM = 128
ROT_DIM = HEAD_DIM // 4
ROPE_THETA = 500000.0
QK_NOPE_DIM = 128
QK_ROPE_DIM = 64
V_HEAD_DIM = 128
MLA_QK_PAD = 256
Q_BLOCK = 128
DILATIONS = (1, 4, 16)
NORM_EPS = 1e-6
NEG_INF = -1e30
LOG2E = 1.4426950408889634
VMEM_LIMIT = 56 * 1024 * 1024


def _rms(x, gain):
    ms = jnp.mean(x * x, axis=-1, keepdims=True)
    return x * lax.rsqrt(ms + NORM_EPS) * gain


def _rope_tables(pos, inv_freq, half):
    lane = lax.broadcasted_iota(jnp.int32, (1, LANES), 1)
    ang = pos * inv_freq
    sign = jnp.where(lane < half, -1.0, 1.0).astype(F32)
    return jnp.cos(ang), jnp.sin(ang) * sign


def _rope(r, cos, sin, half):
    lane = lax.broadcasted_iota(jnp.int32, (1, LANES), 1)
    partner = jnp.where(lane < half,
                        pltpu.roll(r, LANES - half, 1),
                        pltpu.roll(r, half, 1))
    return r * cos + partner * sin


ROW_SPLIT = 2


def _inproj_kernel(x_ref, pos_ref, g_ref, w_ref, invfa_ref, invfb_ref, gq_ref, gkv_ref,
                   aq_ref, ak_ref, av_ref, cq_ref, ckv_ref, kr_ref, cosb_ref, sinb_ref, *,
                   a_width, q_rank, kv_rank):
    n_heads = a_width // HEAD_DIM
    q_scale = HEAD_DIM ** -0.5 * LOG2E
    tm = x_ref.shape[0]
    for part in range(ROW_SPLIT):
        rows = slice(part * tm // ROW_SPLIT, (part + 1) * tm // ROW_SPLIT)
        h = _rms(x_ref[rows, :], g_ref[...]).astype(BF16)
        pos = pos_ref[rows, :]
        cos_a, sin_a = _rope_tables(pos, invfa_ref[...], ROT_DIM // 2)
        cos_b, sin_b = _rope_tables(pos, invfb_ref[...], QK_ROPE_DIM // 2)
        cosb_ref[rows, :] = cos_b
        sinb_ref[rows, :] = sin_b

        r = jnp.dot(h, w_ref[:, 0:a_width], preferred_element_type=F32)
        for hd in range(n_heads):
            sl = slice(hd * HEAD_DIM, (hd + 1) * HEAD_DIM)
            aq_ref[hd, rows, :] = _rope(r[:, sl], cos_a, sin_a, ROT_DIM // 2) * q_scale
        r = jnp.dot(h, w_ref[:, a_width:2 * a_width], preferred_element_type=F32)
        for hd in range(n_heads):
            sl = slice(hd * HEAD_DIM, (hd + 1) * HEAD_DIM)
            ak_ref[hd, rows, :] = _rope(r[:, sl], cos_a, sin_a, ROT_DIM // 2)
        r = jnp.dot(h, w_ref[:, 2 * a_width:3 * a_width], preferred_element_type=F32)
        for hd in range(n_heads):
            av_ref[hd, rows, :] = r[:, hd * HEAD_DIM:(hd + 1) * HEAD_DIM]

        c0 = 3 * a_width
        c1 = c0 + q_rank + kv_rank
        r = jnp.dot(h, w_ref[:, c0:c1], preferred_element_type=F32)
        cq_ref[rows, :] = _rms(r[:, 0:q_rank], gq_ref[...]).astype(BF16)
        ckv_ref[rows, :] = _rms(r[:, q_rank:q_rank + kv_rank], gkv_ref[...]).astype(BF16)
        kr = jnp.dot(h, w_ref[:, c1:], preferred_element_type=F32)
        kr = jnp.concatenate([kr, jnp.zeros((kr.shape[0], LANES - kr.shape[1]), F32)], axis=1)
        kr_ref[rows, :] = _rope(kr, cos_b, sin_b, QK_ROPE_DIM // 2).astype(BF16)


def _inproj(x2d, pos, g_pre, w_in, invfa, invfb, gq, gkv, *, batch, a_width, q_rank, kv_rank, tm):
    T, D = x2d.shape
    S = T // batch
    assert S % tm == 0
    tiles = S // tm
    n_heads = a_width // HEAD_DIM
    ncols = w_in.shape[1]
    const = lambda i: (0, 0)
    row = lambda i: (i, 0)
    head_major = pl.BlockSpec((None, n_heads, tm, HEAD_DIM),
                              lambda i: (i // tiles, 0, i % tiles, 0))
    head_shape = jax.ShapeDtypeStruct((batch, n_heads, S, HEAD_DIM), F32)
    kern = functools.partial(_inproj_kernel, a_width=a_width, q_rank=q_rank, kv_rank=kv_rank)
    return pl.pallas_call(
        kern,
        grid=(T // tm,),
        in_specs=[
            pl.BlockSpec((tm, D), row),
            pl.BlockSpec((tm, 1), row),
            pl.BlockSpec((1, D), const),
            pl.BlockSpec((D, ncols), const, pipeline_mode=pl.Buffered(1)),
            pl.BlockSpec((1, LANES), const),
            pl.BlockSpec((1, LANES), const),
            pl.BlockSpec((1, q_rank), const),
            pl.BlockSpec((1, kv_rank), const),
        ],
        out_specs=[
            head_major, head_major, head_major,
            pl.BlockSpec((tm, q_rank), row),
            pl.BlockSpec((tm, kv_rank), row),
            pl.BlockSpec((tm, LANES), row),
            pl.BlockSpec((tm, LANES), row),
            pl.BlockSpec((tm, LANES), row),
        ],
        out_shape=[
            head_shape, head_shape, head_shape,
            jax.ShapeDtypeStruct((T, q_rank), BF16),
            jax.ShapeDtypeStruct((T, kv_rank), BF16),
            jax.ShapeDtypeStruct((T, LANES), BF16),
            jax.ShapeDtypeStruct((T, LANES), F32),
            jax.ShapeDtypeStruct((T, LANES), F32),
        ],
        compiler_params=pltpu.CompilerParams(
            dimension_semantics=("parallel",), vmem_limit_bytes=VMEM_LIMIT),
        name="inproj",
    )(x2d, pos, g_pre, w_in, invfa, invfb, gq, gkv)


def _mla_up_kernel(cq_ref, ckv_ref, kr_ref, cosb_ref, sinb_ref, wq_ref, wkv_ref,
                   q_ref, k_ref, v_ref, *, n_heads):
    scale = (QK_NOPE_DIM + QK_ROPE_DIM) ** -0.5 * LOG2E
    tm = cq_ref.shape[0]
    for part in range(ROW_SPLIT):
        rows = slice(part * tm // ROW_SPLIT, (part + 1) * tm // ROW_SPLIT)
        cos_b, sin_b = cosb_ref[rows, :], sinb_ref[rows, :]
        rq = jnp.dot(cq_ref[rows, :], wq_ref[...], preferred_element_type=F32)
        rkv = jnp.dot(ckv_ref[rows, :], wkv_ref[...], preferred_element_type=F32)
        kr = kr_ref[rows, :]
        for hd in range(n_heads):
            base = hd * MLA_QK_PAD
            q_ref[hd, rows, 0:LANES] = (rq[:, base:base + LANES] * scale).astype(BF16)
            q_rot = _rope(rq[:, base + LANES:base + 2 * LANES], cos_b, sin_b, QK_ROPE_DIM // 2)
            q_ref[hd, rows, LANES:2 * LANES] = (q_rot * scale).astype(BF16)
            kv0 = hd * (QK_NOPE_DIM + V_HEAD_DIM)
            k_ref[hd, rows, 0:LANES] = rkv[:, kv0:kv0 + QK_NOPE_DIM].astype(BF16)
            k_ref[hd, rows, LANES:2 * LANES] = kr
            v_ref[hd, rows, :] = rkv[:, kv0 + QK_NOPE_DIM:kv0 + QK_NOPE_DIM + V_HEAD_DIM].astype(BF16)


def _mla_up(cq, ckv, kr, cos_b, sin_b, wq, wkv, *, batch, n_heads, tm):
    T = cq.shape[0]
    S = T // batch
    assert S % tm == 0
    tiles = S // tm
    const = lambda i: (0, 0)
    row = lambda i: (i, 0)
    head_major = lambda width: pl.BlockSpec((None, n_heads, tm, width),
                                            lambda i: (i // tiles, 0, i % tiles, 0))
    head_shape = lambda width: jax.ShapeDtypeStruct((batch, n_heads, S, width), BF16)
    return pl.pallas_call(
        functools.partial(_mla_up_kernel, n_heads=n_heads),
        grid=(T // tm,),
        in_specs=[
            pl.BlockSpec((tm, cq.shape[1]), row),
            pl.BlockSpec((tm, ckv.shape[1]), row),
            pl.BlockSpec((tm, LANES), row),
            pl.BlockSpec((tm, LANES), row),
            pl.BlockSpec((tm, LANES), row),
            pl.BlockSpec(wq.shape, const),
            pl.BlockSpec(wkv.shape, const),
        ],
        out_specs=[head_major(MLA_QK_PAD), head_major(MLA_QK_PAD), head_major(V_HEAD_DIM)],
        out_shape=[head_shape(MLA_QK_PAD), head_shape(MLA_QK_PAD), head_shape(V_HEAD_DIM)],
        compiler_params=pltpu.CompilerParams(
            dimension_semantics=("parallel",), vmem_limit_bytes=VMEM_LIMIT),
        name="mla_up",
    )(cq, ckv, kr, cos_b, sin_b, wq, wkv)


MAX_DIL = 16
NT_DIMS = (((1,), (1,)), ((), ()))


def _softmax_block(q, k, v, bias, prev):
    nk = k.shape[0]
    s = lax.dot_general(q, k, NT_DIMS, preferred_element_type=F32)
    blocks = [s[:, i * LANES:(i + 1) * LANES] for i in range(nk // LANES)]
    if bias is not None:
        nb = bias.shape[1] // LANES
        blocks[-nb:] = [b + bias[:, i * LANES:(i + 1) * LANES]
                        for i, b in enumerate(blocks[-nb:])]
    m_cur = jnp.max(functools.reduce(jnp.maximum, blocks), axis=1, keepdims=True)
    if prev is None:
        m_new = jnp.broadcast_to(m_cur, (q.shape[0], LANES))
    else:
        m_prev, l_prev, acc_prev = prev
        m_new = jnp.maximum(m_prev, m_cur)
        alpha = jnp.exp2(m_prev - m_new)
    p = jnp.concatenate([jnp.exp2(b - m_new) for b in blocks], axis=1).astype(BF16)
    v_aug = jnp.concatenate([v, jnp.ones((nk, LANES), BF16)], axis=1)
    pv = jnp.dot(p, v_aug, preferred_element_type=F32)
    acc_cur, l_cur = pv[:, :LANES], pv[:, LANES:]
    if prev is None:
        return m_new, l_cur, acc_cur
    return m_new, alpha * l_prev + l_cur, alpha * acc_prev + acc_cur


def _dswa_kernel(q_ref, k_ref, v_ref, o_ref, xq, xk, xv, mx, lx, ax, mn, ln, an, x4, *, seq):
    U = seq // MAX_DIL
    QB = Q_BLOCK
    NEG = NEG_INF

    def band_bias(dist):
        return jnp.where(dist >= 0, jnp.where(dist <= QB, 0.0, NEG), NEG).astype(F32)

    i1 = lax.broadcasted_iota(jnp.int32, (QB, QB), 0)
    j1 = lax.broadcasted_iota(jnp.int32, (QB, QB), 1)
    i2 = lax.broadcasted_iota(jnp.int32, (QB, 2 * QB), 0)
    j2 = lax.broadcasted_iota(jnp.int32, (QB, 2 * QB), 1)
    bias_first = band_bias(i1 - j1)
    bias_band = band_bias(i2 + QB - j2)
    qc = QB // 4
    lo = lambda x, n: x & (n - 1)
    hi = lambda x, n: x >> (n.bit_length() - 1)
    d4_first = band_bias(4 * (lo(i1, qc) - lo(j1, qc)) + hi(i1, qc) - hi(j1, qc))
    d4_band = band_bias(4 * (lo(i2, qc) - lo(j2, 2 * qc) + qc) + hi(i2, qc) - hi(j2, 2 * qc))

    for src, dst in ((q_ref, xq), (k_ref, xk), (v_ref, xv)):
        for r4 in range(4):
            x4[r4] = src[pl.ds(r4, 4 * U, stride=4), :]
        for r in range(MAX_DIL):
            dst[r] = x4[r % 4, pl.ds(r // 4, U, stride=4), :].astype(BF16)

    for r in range(MAX_DIL):
        for n in range(U // QB):
            qs = slice(n * QB, (n + 1) * QB)
            ks = qs if n == 0 else slice((n - 1) * QB, (n + 1) * QB)
            res = _softmax_block(xq[r, qs, :], xk[r, ks, :], xv[r, ks, :],
                                 bias_first if n == 0 else bias_band, None)
            for ref, val in zip((mx, lx, ax), res):
                ref[r, qs, :] = val

    def d4_tile(r4, a, first):
        def gather(ref, start, size):
            return jnp.concatenate([ref[4 * qq + r4, start:start + size, :] for qq in range(4)],
                                   axis=0)
        q = gather(xq, a, qc)
        prev = (gather(mx, a, qc), gather(lx, a, qc), gather(ax, a, qc))
        if first:
            k, v, bias = gather(xk, a, qc), gather(xv, a, qc), d4_first
        else:
            k, v, bias = gather(xk, a - qc, 2 * qc), gather(xv, a - qc, 2 * qc), d4_band
        return _softmax_block(q, k, v, bias, prev)

    for a in range(0, U, qc):
        for r4 in range(4):
            res = d4_tile(r4, a, a == 0)
            for ref, val in zip((mx, lx, ax), res):
                for qq in range(4):
                    ref[4 * qq + r4, a:a + qc, :] = val[qq * qc:(qq + 1) * qc]

    for src, dst in ((mx, mn), (lx, ln), (ax, an)):
        for r in range(MAX_DIL):
            x4[r % 4, pl.ds(r // 4, U, stride=4), :] = src[r]
        for r4 in range(4):
            dst[pl.ds(r4, 4 * U, stride=4), :] = x4[r4]

    for n in range(seq // QB):
        qs = slice(n * QB, (n + 1) * QB)
        ks = qs if n == 0 else slice((n - 1) * QB, (n + 1) * QB)
        prev = (mn[qs, :], ln[qs, :], an[qs, :])
        _, l_new, acc_new = _softmax_block(
            q_ref[qs, :].astype(BF16), k_ref[ks, :].astype(BF16), v_ref[ks, :].astype(BF16),
            bias_first if n == 0 else bias_band, prev)
        o_ref[qs, :] = (acc_new / l_new).astype(o_ref.dtype)


def _dswa(aq, ak, av):
    B, n_heads, S, _ = aq.shape
    U = S // MAX_DIL
    assert DILATIONS == (1, 4, MAX_DIL) and U % Q_BLOCK == 0
    spec = pl.BlockSpec((None, None, S, HEAD_DIM), lambda b, h: (b, h, 0, 0))
    x16 = lambda dt: pltpu.VMEM((MAX_DIL, U, LANES), dt)
    return pl.pallas_call(
        functools.partial(_dswa_kernel, seq=S),
        grid=(B, n_heads),
        in_specs=[spec, spec, spec],
        out_specs=pl.BlockSpec((None, S, HEAD_DIM), lambda b, h: (b, 0, h)),
        out_shape=jax.ShapeDtypeStruct((B, S, n_heads * HEAD_DIM), BF16),
        scratch_shapes=([x16(BF16)] * 3 + [x16(F32)] * 3 + [pltpu.VMEM((S, LANES), F32)] * 3
                        + [pltpu.VMEM((4, S // 4, LANES), F32)]),
        compiler_params=pltpu.CompilerParams(
            dimension_semantics=("parallel", "parallel"), vmem_limit_bytes=VMEM_LIMIT),
        name="dswa",
    )(aq, ak, av)


MLA_CHAIN = 256


def _cast_specs(weights, n_steps, index):
    specs, shapes = [], []
    for w in weights:
        rows = w.shape[0] // n_steps
        assert rows * n_steps == w.shape[0] and rows % 16 == 0
        specs.append(pl.BlockSpec((rows, w.shape[1]), index))
        shapes.append(jax.ShapeDtypeStruct(w.shape, BF16))
    return specs, shapes


def _mla_attn_kernel(q_ref, k_ref, v_ref, *rest, seq, tk):
    n_cast = (len(rest) - 1) // 2
    o_ref = rest[n_cast]
    for src, dst in zip(rest[:n_cast], rest[n_cast + 1:]):
        dst[...] = src[...].astype(BF16)
    _mla_attn_body(q_ref, k_ref, v_ref, o_ref, seq=seq, tk=tk)


def _mla_attn_body(q_ref, k_ref, v_ref, o_ref, *, seq, tk):
    cq = MLA_CHAIN
    n_chains = seq // cq
    ri = lax.broadcasted_iota(jnp.int32, (cq, cq), 0)
    ci = lax.broadcasted_iota(jnp.int32, (cq, cq), 1)
    diag_bias = jnp.where(ci <= ri, 0.0, NEG_INF).astype(F32)

    plans = []
    for c in range(n_chains):
        n_full = (c * cq) // tk
        plan = [(j * tk, tk, None) for j in range(n_full)]
        plan.append((n_full * tk, (c + 1) * cq - n_full * tk, diag_bias))
        plans.append(plan)

    state = [None] * n_chains
    for t in range(max(len(p) for p in plans)):
        for c in reversed(range(n_chains)):
            if t >= len(plans[c]):
                continue
            start, nk, bias = plans[c][t]
            rows = slice(c * cq, (c + 1) * cq)
            ks = slice(start, start + nk)
            state[c] = _softmax_block(q_ref[rows, :], k_ref[ks, :], v_ref[ks, :], bias, state[c])
            if t == len(plans[c]) - 1:
                _, l, acc = state[c]
                o_ref[rows, :] = (acc / l).astype(o_ref.dtype)


def _mla_attn(q, k, v, cast_weights, *, tk):
    B, n_heads, S, _ = q.shape
    assert tk % MLA_CHAIN == 0 and S % tk == 0
    head = lambda width: pl.BlockSpec((None, None, S, width), lambda b, h: (b, h, 0, 0))
    cast_specs, cast_shapes = _cast_specs(cast_weights, B * n_heads,
                                          lambda b, h: (b * n_heads + h, 0))
    return pl.pallas_call(
        functools.partial(_mla_attn_kernel, seq=S, tk=tk),
        grid=(B, n_heads),
        in_specs=[head(MLA_QK_PAD), head(MLA_QK_PAD), head(V_HEAD_DIM)] + cast_specs,
        out_specs=[pl.BlockSpec((None, S, V_HEAD_DIM), lambda b, h: (b, 0, h))] + cast_specs,
        out_shape=[jax.ShapeDtypeStruct((B, S, n_heads * V_HEAD_DIM), BF16)] + cast_shapes,
        compiler_params=pltpu.CompilerParams(
            dimension_semantics=("parallel", "parallel"), vmem_limit_bytes=VMEM_LIMIT),
        name="mla_attn",
    )(q, k, v, *cast_weights)


def _outproj_kernel(a_ref, b_ref, x_ref, w_ref, g_ref, o_ref, *, a_width):
    tm = x_ref.shape[0]
    for part in range(ROW_SPLIT):
        rows = slice(part * tm // ROW_SPLIT, (part + 1) * tm // ROW_SPLIT)
        y = jnp.dot(a_ref[rows, :], w_ref[0:a_width, :], preferred_element_type=F32)
        y = y + jnp.dot(b_ref[rows, :], w_ref[a_width:, :], preferred_element_type=F32)
        o_ref[rows, :] = x_ref[rows, :] + _rms(y, g_ref[...])


def _outproj(a, b, x2d, w_out, g_post, *, tm):
    T, D = x2d.shape
    const = lambda i: (0, 0)
    row = lambda i: (i, 0)
    return pl.pallas_call(
        functools.partial(_outproj_kernel, a_width=a.shape[1]),
        grid=(T // tm,),
        in_specs=[
            pl.BlockSpec((tm, a.shape[1]), row),
            pl.BlockSpec((tm, b.shape[1]), row),
            pl.BlockSpec((tm, D), row),
            pl.BlockSpec(w_out.shape, const, pipeline_mode=pl.Buffered(1)),
            pl.BlockSpec((1, D), const),
        ],
        out_specs=pl.BlockSpec((tm, D), row),
        out_shape=jax.ShapeDtypeStruct((T, D), F32),
        compiler_params=pltpu.CompilerParams(
            dimension_semantics=("parallel",), vmem_limit_bytes=VMEM_LIMIT),
        name="outproj",
    )(a, b, x2d, w_out, g_post)


def _mlp_kernel(x_ref, gpre_ref, gpost_ref, wu_ref, wd_ref, o_ref, h_sc):
    f = pl.program_id(1)
    last = pl.num_programs(1) - 1
    tm = x_ref.shape[0]

    def step(first, final):
        for part in range(ROW_SPLIT):
            rows = slice(part * tm // ROW_SPLIT, (part + 1) * tm // ROW_SPLIT)
            if first:
                h = _rms(x_ref[rows, :], gpre_ref[...]).astype(BF16)
                h_sc[rows, :] = h
            else:
                h = h_sc[rows, :]
            u = jnp.dot(h, wu_ref[...], preferred_element_type=F32)
            u = jnp.square(jnp.maximum(u, 0.0)).astype(BF16)
            y = jnp.dot(u, wd_ref[...], preferred_element_type=F32)
            acc = y if first else o_ref[rows, :] + y
            if final:
                o_ref[rows, :] = x_ref[rows, :] + _rms(acc, gpost_ref[...])
            else:
                o_ref[rows, :] = acc

    pl.when(f == 0)(lambda: step(True, False))
    pl.when(jnp.logical_and(f > 0, f < last))(lambda: step(False, False))
    pl.when(f == last)(lambda: step(False, True))


def _mlp(x2d, g_pre, g_post, w_up, w_down, *, tm, tf):
    T, D = x2d.shape
    F = w_up.shape[1]
    return pl.pallas_call(
        _mlp_kernel,
        grid=(T // tm, F // tf),
        in_specs=[
            pl.BlockSpec((tm, D), lambda i, f: (i, 0)),
            pl.BlockSpec((1, D), lambda i, f: (0, 0)),
            pl.BlockSpec((1, D), lambda i, f: (0, 0)),
            pl.BlockSpec((D, tf), lambda i, f: (0, f)),
            pl.BlockSpec((tf, D), lambda i, f: (f, 0)),
        ],
        out_specs=pl.BlockSpec((tm, D), lambda i, f: (i, 0)),
        out_shape=jax.ShapeDtypeStruct((T, D), F32),
        scratch_shapes=[pltpu.VMEM((tm, D), BF16)],
        compiler_params=pltpu.CompilerParams(
            dimension_semantics=("parallel", "arbitrary"), vmem_limit_bytes=VMEM_LIMIT),
        name="mlp",
    )(x2d, g_pre, g_post, w_up, w_down)


def _inv_freq_lanes(rot_dim):
    inv = ROPE_THETA ** (-jnp.arange(0, rot_dim, 2, dtype=F32) / rot_dim)
    lanes = jnp.concatenate([inv, inv, jnp.zeros((LANES - rot_dim,), F32)])
    return lanes.reshape(1, LANES)


def _layer(x2d, pos, B, S, norm_attn_pre, norm_attn_post, w_in, q_latent_norm, kv_latent_norm,
           w_uq, w_ukv, w_out, norm_mlp_pre, norm_mlp_post, w_up, w_down):
    D = x2d.shape[1]
    q_rank = q_latent_norm.shape[0]
    kv_rank = kv_latent_norm.shape[0]
    n_mla = w_uq.shape[1] // (QK_NOPE_DIM + QK_ROPE_DIM)
    a_width = (w_in.shape[1] - q_rank - kv_rank - QK_ROPE_DIM) // 3

    w_in_p = w_in.astype(BF16)
    wq3 = w_uq.reshape(q_rank, n_mla, QK_NOPE_DIM + QK_ROPE_DIM)
    wq_p = jnp.pad(wq3, ((0, 0), (0, 0), (0, MLA_QK_PAD - QK_NOPE_DIM - QK_ROPE_DIM)))
    wq_p = wq_p.reshape(q_rank, n_mla * MLA_QK_PAD).astype(BF16)
    wkv_p = w_ukv.astype(BF16)

    invfa = _inv_freq_lanes(ROT_DIM)
    invfb = _inv_freq_lanes(QK_ROPE_DIM)
    row = lambda g: g.reshape(1, -1)

    aq, ak, av, cq, ckv, kr, cos_b, sin_b = _inproj(
        x2d, pos, row(norm_attn_pre), w_in_p, invfa, invfb, row(q_latent_norm),
        row(kv_latent_norm), batch=B, a_width=a_width, q_rank=q_rank, kv_rank=kv_rank, tm=512)
    q_b, k_b, v_b = _mla_up(cq, ckv, kr, cos_b, sin_b, wq_p, wkv_p, batch=B, n_heads=n_mla,
                            tm=512)

    a_out = _dswa(aq, ak, av)
    b_out, w_out_b, w_up_b, w_down_b = _mla_attn(q_b, k_b, v_b, [w_out, w_up, w_down], tk=512)

    x2d = _outproj(a_out.reshape(B * S, -1), b_out.reshape(B * S, -1), x2d,
                   w_out_b, row(norm_attn_post), tm=512)
    return _mlp(x2d, row(norm_mlp_pre), row(norm_mlp_post), w_up_b, w_down_b, tm=512, tf=2048)


def kernel(x, positions, norm_attn_pre, norm_attn_post, w_in, q_latent_norm, kv_latent_norm,
           w_uq, w_ukv, w_out, norm_mlp_pre, norm_mlp_post, w_up, w_down):
    B, S, D = x.shape
    x2d = x.reshape(B * S, D)
    pos = positions.astype(F32).reshape(B * S, 1)
    for layer in range(w_in.shape[0]):
        x2d = _layer(x2d, pos, B, S, norm_attn_pre[layer], norm_attn_post[layer], w_in[layer],
                     q_latent_norm[layer], kv_latent_norm[layer], w_uq[layer], w_ukv[layer],
                     w_out[layer], norm_mlp_pre[layer], norm_mlp_post[layer], w_up[layer],
                     w_down[layer])
    return x2d.reshape(B, S, D)
```

```python
import functools

import jax
import jax.numpy as jnp
from jax import lax
from jax.experimental import pallas as pl
from jax.experimental.pallas import tpu as pltpu

F32 = jnp.float32
BF16 = jnp.bfloat16

LANES = 128
HEAD_DIM = 128
ROT_DIM = HEAD_DIM // 4
ROPE_THETA = 500000.0
QK_NOPE_DIM = 128
QK_ROPE_DIM = 64
V_HEAD_DIM = 128
MLA_QK_PAD = 256
Q_BLOCK = 128
DILATIONS = (1, 4, 16)
NORM_EPS = 1e-6
NEG_INF = -1e30
LOG2E = 1.4426950408889634
VMEM_LIMIT = 56 * 1024 * 1024


def _rms(x, gain):
    ms = jnp.mean(x * x, axis=-1, keepdims=True)
    return x * lax.rsqrt(ms + NORM_EPS) * gain


def _rope_tables(pos, inv_freq, half):
    lane = lax.broadcasted_iota(jnp.int32, (1, LANES), 1)
    ang = pos * inv_freq
    sign = jnp.where(lane < half, -1.0, 1.0).astype(F32)
    return jnp.cos(ang), jnp.sin(ang) * sign


def _rope(r, cos, sin, half):
    lane = lax.broadcasted_iota(jnp.int32, (1, LANES), 1)
    partner = jnp.where(lane < half,
                        pltpu.roll(r, LANES - half, 1),
                        pltpu.roll(r, half, 1))
    return r * cos + partner * sin


ROW_SPLIT = 2


def _inproj_kernel(x_ref, pos_ref, g_ref, w_ref, invfa_ref, invfb_ref, gq_ref, gkv_ref,
                   aq_ref, ak_ref, av_ref, cq_ref, ckv_ref, kr_ref, cosb_ref, sinb_ref, *,
                   a_width, q_rank, kv_rank):
    n_heads = a_width // HEAD_DIM
    q_scale = HEAD_DIM ** -0.5 * LOG2E
    tm = x_ref.shape[0]
    for part in range(ROW_SPLIT):
        rows = slice(part * tm // ROW_SPLIT, (part + 1) * tm // ROW_SPLIT)
        h = _rms(x_ref[rows, :], g_ref[...]).astype(BF16)
        pos = pos_ref[rows, :]
        cos_a, sin_a = _rope_tables(pos, invfa_ref[...], ROT_DIM // 2)
        cos_b, sin_b = _rope_tables(pos, invfb_ref[...], QK_ROPE_DIM // 2)
        cosb_ref[rows, :] = cos_b
        sinb_ref[rows, :] = sin_b

        r = jnp.dot(h, w_ref[:, 0:a_width], preferred_element_type=F32)
        for hd in range(n_heads):
            sl = slice(hd * HEAD_DIM, (hd + 1) * HEAD_DIM)
            aq_ref[hd, rows, :] = _rope(r[:, sl], cos_a, sin_a, ROT_DIM // 2) * q_scale
        r = jnp.dot(h, w_ref[:, a_width:2 * a_width], preferred_element_type=F32)
        for hd in range(n_heads):
            sl = slice(hd * HEAD_DIM, (hd + 1) * HEAD_DIM)
            ak_ref[hd, rows, :] = _rope(r[:, sl], cos_a, sin_a, ROT_DIM // 2)
        r = jnp.dot(h, w_ref[:, 2 * a_width:3 * a_width], preferred_element_type=F32)
        for hd in range(n_heads):
            av_ref[hd, rows, :] = r[:, hd * HEAD_DIM:(hd + 1) * HEAD_DIM]

        c0 = 3 * a_width
        c1 = c0 + q_rank + kv_rank
        r = jnp.dot(h, w_ref[:, c0:c1], preferred_element_type=F32)
        cq_ref[rows, :] = _rms(r[:, 0:q_rank], gq_ref[...]).astype(BF16)
        ckv_ref[rows, :] = _rms(r[:, q_rank:q_rank + kv_rank], gkv_ref[...]).astype(BF16)
        kr = jnp.dot(h, w_ref[:, c1:], preferred_element_type=F32)
        kr = jnp.concatenate([kr, jnp.zeros((kr.shape[0], LANES - kr.shape[1]), F32)], axis=1)
        kr_ref[rows, :] = _rope(kr, cos_b, sin_b, QK_ROPE_DIM // 2).astype(BF16)


def _inproj(x2d, pos, g_pre, w_in, invfa, invfb, gq, gkv, *, batch, a_width, q_rank, kv_rank, tm):
    T, D = x2d.shape
    S = T // batch
    assert S % tm == 0
    tiles = S // tm
    n_heads = a_width // HEAD_DIM
    ncols = w_in.shape[1]
    const = lambda i: (0, 0)
    row = lambda i: (i, 0)
    head_major = pl.BlockSpec((None, n_heads, tm, HEAD_DIM),
                              lambda i: (i // tiles, 0, i % tiles, 0))
    head_shape = jax.ShapeDtypeStruct((batch, n_heads, S, HEAD_DIM), F32)
    kern = functools.partial(_inproj_kernel, a_width=a_width, q_rank=q_rank, kv_rank=kv_rank)
    return pl.pallas_call(
        kern,
        grid=(T // tm,),
        in_specs=[
            pl.BlockSpec((tm, D), row),
            pl.BlockSpec((tm, 1), row),
            pl.BlockSpec((1, D), const),
            pl.BlockSpec((D, ncols), const, pipeline_mode=pl.Buffered(1)),
            pl.BlockSpec((1, LANES), const),
            pl.BlockSpec((1, LANES), const),
            pl.BlockSpec((1, q_rank), const),
            pl.BlockSpec((1, kv_rank), const),
        ],
        out_specs=[
            head_major, head_major, head_major,
            pl.BlockSpec((tm, q_rank), row),
            pl.BlockSpec((tm, kv_rank), row),
            pl.BlockSpec((tm, LANES), row),
            pl.BlockSpec((tm, LANES), row),
            pl.BlockSpec((tm, LANES), row),
        ],
        out_shape=[
            head_shape, head_shape, head_shape,
            jax.ShapeDtypeStruct((T, q_rank), BF16),
            jax.ShapeDtypeStruct((T, kv_rank), BF16),
            jax.ShapeDtypeStruct((T, LANES), BF16),
            jax.ShapeDtypeStruct((T, LANES), F32),
            jax.ShapeDtypeStruct((T, LANES), F32),
        ],
        compiler_params=pltpu.CompilerParams(
            dimension_semantics=("parallel",), vmem_limit_bytes=VMEM_LIMIT),
        name="inproj",
    )(x2d, pos, g_pre, w_in, invfa, invfb, gq, gkv)


def _mla_up_kernel(cq_ref, ckv_ref, kr_ref, cosb_ref, sinb_ref, wq_ref, wkv_ref,
                   q_ref, k_ref, v_ref, *, n_heads):
    scale = (QK_NOPE_DIM + QK_ROPE_DIM) ** -0.5 * LOG2E
    tm = cq_ref.shape[0]
    for part in range(ROW_SPLIT):
        rows = slice(part * tm // ROW_SPLIT, (part + 1) * tm // ROW_SPLIT)
        cos_b, sin_b = cosb_ref[rows, :], sinb_ref[rows, :]
        rq = jnp.dot(cq_ref[rows, :], wq_ref[...], preferred_element_type=F32)
        rkv = jnp.dot(ckv_ref[rows, :], wkv_ref[...], preferred_element_type=F32)
        kr = kr_ref[rows, :]
        rope_pad = jnp.zeros((rq.shape[0], LANES - QK_ROPE_DIM), F32)
        for hd in range(n_heads):
            base = hd * (QK_NOPE_DIM + QK_ROPE_DIM)
            q_ref[hd, rows, 0:LANES] = (rq[:, base:base + QK_NOPE_DIM] * scale).astype(BF16)
            q_rope = jnp.concatenate(
                [rq[:, base + QK_NOPE_DIM:base + QK_NOPE_DIM + QK_ROPE_DIM], rope_pad], axis=1)
            q_rot = _rope(q_rope, cos_b, sin_b, QK_ROPE_DIM // 2)
            q_ref[hd, rows, LANES:2 * LANES] = (q_rot * scale).astype(BF16)
            kv0 = hd * (QK_NOPE_DIM + V_HEAD_DIM)
            k_ref[hd, rows, 0:LANES] = rkv[:, kv0:kv0 + QK_NOPE_DIM].astype(BF16)
            k_ref[hd, rows, LANES:2 * LANES] = kr
            v_ref[hd, rows, :] = rkv[:, kv0 + QK_NOPE_DIM:kv0 + QK_NOPE_DIM + V_HEAD_DIM].astype(BF16)


def _mla_up(cq, ckv, kr, cos_b, sin_b, wq, wkv, *, batch, n_heads, tm):
    T = cq.shape[0]
    S = T // batch
    assert S % tm == 0
    tiles = S // tm
    const = lambda i: (0, 0)
    row = lambda i: (i, 0)
    head_major = lambda width: pl.BlockSpec((None, n_heads, tm, width),
                                            lambda i: (i // tiles, 0, i % tiles, 0))
    head_shape = lambda width: jax.ShapeDtypeStruct((batch, n_heads, S, width), BF16)
    return pl.pallas_call(
        functools.partial(_mla_up_kernel, n_heads=n_heads),
        grid=(T // tm,),
        in_specs=[
            pl.BlockSpec((tm, cq.shape[1]), row),
            pl.BlockSpec((tm, ckv.shape[1]), row),
            pl.BlockSpec((tm, LANES), row),
            pl.BlockSpec((tm, LANES), row),
            pl.BlockSpec((tm, LANES), row),
            pl.BlockSpec(wq.shape, const),
            pl.BlockSpec(wkv.shape, const),
        ],
        out_specs=[head_major(MLA_QK_PAD), head_major(MLA_QK_PAD), head_major(V_HEAD_DIM)],
        out_shape=[head_shape(MLA_QK_PAD), head_shape(MLA_QK_PAD), head_shape(V_HEAD_DIM)],
        compiler_params=pltpu.CompilerParams(
            dimension_semantics=("parallel",), vmem_limit_bytes=VMEM_LIMIT),
        name="mla_up",
    )(cq, ckv, kr, cos_b, sin_b, wq, wkv)


MAX_DIL = 16
NT_DIMS = (((1,), (1,)), ((), ()))


def _softmax_block(q, k, v, bias, prev):
    nk = k.shape[0]
    s = lax.dot_general(q, k, NT_DIMS, preferred_element_type=F32)
    blocks = [s[:, i * LANES:(i + 1) * LANES] for i in range(nk // LANES)]
    if bias is not None:
        nb = bias.shape[1] // LANES
        blocks[-nb:] = [b + bias[:, i * LANES:(i + 1) * LANES]
                        for i, b in enumerate(blocks[-nb:])]
    m_cur = jnp.max(functools.reduce(jnp.maximum, blocks), axis=1, keepdims=True)
    if prev is None:
        m_new = jnp.broadcast_to(m_cur, (q.shape[0], LANES))
    else:
        m_prev, l_prev, acc_prev = prev
        m_new = jnp.maximum(m_prev, m_cur)
        alpha = jnp.exp2(m_prev - m_new)
    p = jnp.concatenate([jnp.exp2(b - m_new) for b in blocks], axis=1).astype(BF16)
    v_aug = jnp.concatenate([v, jnp.ones((nk, LANES), BF16)], axis=1)
    pv = jnp.dot(p, v_aug, preferred_element_type=F32)
    acc_cur, l_cur = pv[:, :LANES], pv[:, LANES:]
    if prev is None:
        return m_new, l_cur, acc_cur
    return m_new, alpha * l_prev + l_cur, alpha * acc_prev + acc_cur


def _dswa_kernel(q_ref, k_ref, v_ref, o_ref, xq, xk, xv, mx, lx, ax, mn, ln, an, x4, *, seq):
    U = seq // MAX_DIL
    QB = Q_BLOCK
    NEG = NEG_INF

    def band_bias(dist):
        return jnp.where(dist >= 0, jnp.where(dist <= QB, 0.0, NEG), NEG).astype(F32)

    i1 = lax.broadcasted_iota(jnp.int32, (QB, QB), 0)
    j1 = lax.broadcasted_iota(jnp.int32, (QB, QB), 1)
    i2 = lax.broadcasted_iota(jnp.int32, (QB, 2 * QB), 0)
    j2 = lax.broadcasted_iota(jnp.int32, (QB, 2 * QB), 1)
    bias_first = band_bias(i1 - j1)
    bias_band = band_bias(i2 + QB - j2)
    qc = QB // 4
    lo = lambda x, n: x & (n - 1)
    hi = lambda x, n: x >> (n.bit_length() - 1)
    d4_first = band_bias(4 * (lo(i1, qc) - lo(j1, qc)) + hi(i1, qc) - hi(j1, qc))
    d4_band = band_bias(4 * (lo(i2, qc) - lo(j2, 2 * qc) + qc) + hi(i2, qc) - hi(j2, 2 * qc))

    for src, dst in ((q_ref, xq), (k_ref, xk), (v_ref, xv)):
        for r4 in range(4):
            x4[r4] = src[pl.ds(r4, 4 * U, stride=4), :]
        for r in range(MAX_DIL):
            dst[r] = x4[r % 4, pl.ds(r // 4, U, stride=4), :].astype(BF16)

    for r in range(MAX_DIL):
        for n in range(U // QB):
            qs = slice(n * QB, (n + 1) * QB)
            ks = qs if n == 0 else slice((n - 1) * QB, (n + 1) * QB)
            res = _softmax_block(xq[r, qs, :], xk[r, ks, :], xv[r, ks, :],
                                 bias_first if n == 0 else bias_band, None)
            for ref, val in zip((mx, lx, ax), res):
                ref[r, qs, :] = val

    def d4_tile(r4, a, first):
        def gather(ref, start, size):
            return jnp.concatenate([ref[4 * qq + r4, start:start + size, :] for qq in range(4)],
                                   axis=0)
        q = gather(xq, a, qc)
        prev = (gather(mx, a, qc), gather(lx, a, qc), gather(ax, a, qc))
        if first:
            k, v, bias = gather(xk, a, qc), gather(xv, a, qc), d4_first
        else:
            k, v, bias = gather(xk, a - qc, 2 * qc), gather(xv, a - qc, 2 * qc), d4_band
        return _softmax_block(q, k, v, bias, prev)

    for a in range(0, U, qc):
        for r4 in range(4):
            res = d4_tile(r4, a, a == 0)
            for ref, val in zip((mx, lx, ax), res):
                for qq in range(4):
                    ref[4 * qq + r4, a:a + qc, :] = val[qq * qc:(qq + 1) * qc]

    for src, dst in ((mx, mn), (lx, ln), (ax, an)):
        for r in range(MAX_DIL):
            x4[r % 4, pl.ds(r // 4, U, stride=4), :] = src[r]
        for r4 in range(4):
            dst[pl.ds(r4, 4 * U, stride=4), :] = x4[r4]

    for n in range(seq // QB):
        qs = slice(n * QB, (n + 1) * QB)
        ks = qs if n == 0 else slice((n - 1) * QB, (n + 1) * QB)
        prev = (mn[qs, :], ln[qs, :], an[qs, :])
        _, l_new, acc_new = _softmax_block(
            q_ref[qs, :].astype(BF16), k_ref[ks, :].astype(BF16), v_ref[ks, :].astype(BF16),
            bias_first if n == 0 else bias_band, prev)
        o_ref[qs, :] = (acc_new / l_new).astype(o_ref.dtype)


def _dswa(aq, ak, av):
    B, n_heads, S, _ = aq.shape
    U = S // MAX_DIL
    assert DILATIONS == (1, 4, MAX_DIL) and U % Q_BLOCK == 0
    spec = pl.BlockSpec((None, None, S, HEAD_DIM), lambda b, h: (b, h, 0, 0))
    x16 = lambda dt: pltpu.VMEM((MAX_DIL, U, LANES), dt)
    return pl.pallas_call(
        functools.partial(_dswa_kernel, seq=S),
        grid=(B, n_heads),
        in_specs=[spec, spec, spec],
        out_specs=pl.BlockSpec((None, S, HEAD_DIM), lambda b, h: (b, 0, h)),
        out_shape=jax.ShapeDtypeStruct((B, S, n_heads * HEAD_DIM), BF16),
        scratch_shapes=([x16(BF16)] * 3 + [x16(F32)] * 3 + [pltpu.VMEM((S, LANES), F32)] * 3
                        + [pltpu.VMEM((4, S // 4, LANES), F32)]),
        compiler_params=pltpu.CompilerParams(
            dimension_semantics=("parallel", "parallel"), vmem_limit_bytes=VMEM_LIMIT),
        name="dswa",
    )(aq, ak, av)


MLA_CHAIN = 256


def _cast_specs(weights, n_steps, index):
    specs, shapes = [], []
    for w in weights:
        rows = w.shape[0] // n_steps
        assert rows * n_steps == w.shape[0] and rows % 16 == 0
        specs.append(pl.BlockSpec((rows, w.shape[1]), index))
        shapes.append(jax.ShapeDtypeStruct(w.shape, BF16))
    return specs, shapes


def _mla_attn_kernel(q_ref, k_ref, v_ref, *rest, seq, tk):
    n_cast = (len(rest) - 1) // 2
    o_ref = rest[n_cast]
    for src, dst in zip(rest[:n_cast], rest[n_cast + 1:]):
        dst[...] = src[...].astype(BF16)
    _mla_attn_body(q_ref, k_ref, v_ref, o_ref, seq=seq, tk=tk)


def _mla_attn_body(q_ref, k_ref, v_ref, o_ref, *, seq, tk):
    cq = MLA_CHAIN
    n_chains = seq // cq
    ri = lax.broadcasted_iota(jnp.int32, (cq, cq), 0)
    ci = lax.broadcasted_iota(jnp.int32, (cq, cq), 1)
    diag_bias = jnp.where(ci <= ri, 0.0, NEG_INF).astype(F32)

    plans = []
    for c in range(n_chains):
        n_full = (c * cq) // tk
        plan = [(j * tk, tk, None) for j in range(n_full)]
        plan.append((n_full * tk, (c + 1) * cq - n_full * tk, diag_bias))
        plans.append(plan)

    state = [None] * n_chains
    for t in range(max(len(p) for p in plans)):
        for c in reversed(range(n_chains)):
            if t >= len(plans[c]):
                continue
            start, nk, bias = plans[c][t]
            rows = slice(c * cq, (c + 1) * cq)
            ks = slice(start, start + nk)
            state[c] = _softmax_block(q_ref[rows, :], k_ref[ks, :], v_ref[ks, :], bias, state[c])
            if t == len(plans[c]) - 1:
                _, l, acc = state[c]
                o_ref[rows, :] = (acc / l).astype(o_ref.dtype)


def _mla_attn(q, k, v, cast_weights, *, tk):
    B, n_heads, S, _ = q.shape
    assert tk % MLA_CHAIN == 0 and S % tk == 0
    head = lambda width: pl.BlockSpec((None, None, S, width), lambda b, h: (b, h, 0, 0))
    cast_specs, cast_shapes = _cast_specs(cast_weights, B * n_heads,
                                          lambda b, h: (b * n_heads + h, 0))
    return pl.pallas_call(
        functools.partial(_mla_attn_kernel, seq=S, tk=tk),
        grid=(B, n_heads),
        in_specs=[head(MLA_QK_PAD), head(MLA_QK_PAD), head(V_HEAD_DIM)] + cast_specs,
        out_specs=[pl.BlockSpec((None, S, V_HEAD_DIM), lambda b, h: (b, 0, h))] + cast_specs,
        out_shape=[jax.ShapeDtypeStruct((B, S, n_heads * V_HEAD_DIM), BF16)] + cast_shapes,
        compiler_params=pltpu.CompilerParams(
            dimension_semantics=("parallel", "parallel"), vmem_limit_bytes=VMEM_LIMIT),
        name="mla_attn",
    )(q, k, v, *cast_weights)


def _outproj_kernel(a_ref, b_ref, x_ref, w_ref, g_ref, o_ref, *, a_width):
    tm = x_ref.shape[0]
    for part in range(ROW_SPLIT):
        rows = slice(part * tm // ROW_SPLIT, (part + 1) * tm // ROW_SPLIT)
        y = jnp.dot(a_ref[rows, :], w_ref[0:a_width, :], preferred_element_type=F32)
        y = y + jnp.dot(b_ref[rows, :], w_ref[a_width:, :], preferred_element_type=F32)
        o_ref[rows, :] = x_ref[rows, :] + _rms(y, g_ref[...])


def _outproj(a, b, x2d, w_out, g_post, *, tm):
    T, D = x2d.shape
    const = lambda i: (0, 0)
    row = lambda i: (i, 0)
    return pl.pallas_call(
        functools.partial(_outproj_kernel, a_width=a.shape[1]),
        grid=(T // tm,),
        in_specs=[
            pl.BlockSpec((tm, a.shape[1]), row),
            pl.BlockSpec((tm, b.shape[1]), row),
            pl.BlockSpec((tm, D), row),
            pl.BlockSpec(w_out.shape, const, pipeline_mode=pl.Buffered(1)),
            pl.BlockSpec((1, D), const),
        ],
        out_specs=pl.BlockSpec((tm, D), row),
        out_shape=jax.ShapeDtypeStruct((T, D), F32),
        compiler_params=pltpu.CompilerParams(
            dimension_semantics=("parallel",), vmem_limit_bytes=VMEM_LIMIT),
        name="outproj",
    )(a, b, x2d, w_out, g_post)


def _mlp_kernel(x_ref, gpre_ref, gpost_ref, wu_ref, wd_ref, o_ref, h_sc):
    f = pl.program_id(1)
    last = pl.num_programs(1) - 1
    tm = x_ref.shape[0]

    def step(first, final):
        for part in range(ROW_SPLIT):
            rows = slice(part * tm // ROW_SPLIT, (part + 1) * tm // ROW_SPLIT)
            if first:
                h = _rms(x_ref[rows, :], gpre_ref[...]).astype(BF16)
                h_sc[rows, :] = h
            else:
                h = h_sc[rows, :]
            u = jnp.dot(h, wu_ref[...], preferred_element_type=F32)
            u = jnp.square(jnp.maximum(u, 0.0)).astype(BF16)
            y = jnp.dot(u, wd_ref[...], preferred_element_type=F32)
            acc = y if first else o_ref[rows, :] + y
            if final:
                o_ref[rows, :] = x_ref[rows, :] + _rms(acc, gpost_ref[...])
            else:
                o_ref[rows, :] = acc

    pl.when(f == 0)(lambda: step(True, False))
    pl.when(jnp.logical_and(f > 0, f < last))(lambda: step(False, False))
    pl.when(f == last)(lambda: step(False, True))


def _mlp(x2d, g_pre, g_post, w_up, w_down, *, tm, tf):
    T, D = x2d.shape
    F = w_up.shape[1]
    return pl.pallas_call(
        _mlp_kernel,
        grid=(T // tm, F // tf),
        in_specs=[
            pl.BlockSpec((tm, D), lambda i, f: (i, 0)),
            pl.BlockSpec((1, D), lambda i, f: (0, 0)),
            pl.BlockSpec((1, D), lambda i, f: (0, 0)),
            pl.BlockSpec((D, tf), lambda i, f: (0, f)),
            pl.BlockSpec((tf, D), lambda i, f: (f, 0)),
        ],
        out_specs=pl.BlockSpec((tm, D), lambda i, f: (i, 0)),
        out_shape=jax.ShapeDtypeStruct((T, D), F32),
        scratch_shapes=[pltpu.VMEM((tm, D), BF16)],
        compiler_params=pltpu.CompilerParams(
            dimension_semantics=("parallel", "arbitrary"), vmem_limit_bytes=VMEM_LIMIT),
        name="mlp",
    )(x2d, g_pre, g_post, w_up, w_down)


def _inv_freq_lanes(rot_dim):
    inv = ROPE_THETA ** (-jnp.arange(0, rot_dim, 2, dtype=F32) / rot_dim)
    lanes = jnp.concatenate([inv, inv, jnp.zeros((LANES - rot_dim,), F32)])
    return lanes.reshape(1, LANES)


def _layer(x2d, pos, B, S, norm_attn_pre, norm_attn_post, w_in, q_latent_norm, kv_latent_norm,
           w_uq, w_ukv, w_out, norm_mlp_pre, norm_mlp_post, w_up, w_down):
    D = x2d.shape[1]
    q_rank = q_latent_norm.shape[0]
    kv_rank = kv_latent_norm.shape[0]
    n_mla = w_uq.shape[1] // (QK_NOPE_DIM + QK_ROPE_DIM)
    a_width = (w_in.shape[1] - q_rank - kv_rank - QK_ROPE_DIM) // 3

    w_in_p = w_in.astype(BF16)
    wq_p = w_uq.astype(BF16)
    wkv_p = w_ukv.astype(BF16)

    invfa = _inv_freq_lanes(ROT_DIM)
    invfb = _inv_freq_lanes(QK_ROPE_DIM)
    row = lambda g: g.reshape(1, -1)

    aq, ak, av, cq, ckv, kr, cos_b, sin_b = _inproj(
        x2d, pos, row(norm_attn_pre), w_in_p, invfa, invfb, row(q_latent_norm),
        row(kv_latent_norm), batch=B, a_width=a_width, q_rank=q_rank, kv_rank=kv_rank, tm=512)
    q_b, k_b, v_b = _mla_up(cq, ckv, kr, cos_b, sin_b, wq_p, wkv_p, batch=B, n_heads=n_mla,
                            tm=512)

    a_out = _dswa(aq, ak, av)
    b_out, w_out_b, w_up_b, w_down_b = _mla_attn(q_b, k_b, v_b, [w_out, w_up, w_down], tk=512)

    x2d = _outproj(a_out.reshape(B * S, -1), b_out.reshape(B * S, -1), x2d,
                   w_out_b, row(norm_attn_post), tm=512)
    return _mlp(x2d, row(norm_mlp_pre), row(norm_mlp_post), w_up_b, w_down_b, tm=512, tf=2048)


def kernel(x, positions, norm_attn_pre, norm_attn_post, w_in, q_latent_norm, kv_latent_norm,
           w_uq, w_ukv, w_out, norm_mlp_pre, norm_mlp_post, w_up, w_down):
    B, S, D = x.shape
    x2d = x.reshape(B * S, D)
    pos = positions.astype(F32).reshape(B * S, 1)
    for layer in range(w_in.shape[0]):
        x2d = _layer(x2d, pos, B, S, norm_attn_pre[layer], norm_attn_post[layer], w_in[layer],
                     q_latent_norm[layer], kv_latent_norm[layer], w_uq[layer], w_ukv[layer],
                     w_out[layer], norm_mlp_pre[layer], norm_mlp_post[layer], w_up[layer],
                     w_down[layer])
    return x2d.reshape(B, S, D)
```

```python
import functools

import jax
import jax.numpy as jnp
from jax import lax
from jax.experimental import pallas as pl
from jax.experimental.pallas import tpu as pltpu

F32 = jnp.float32
BF16 = jnp.bfloat16

LANES = 128
HEAD_DIM = 128
ROT_DIM = HEAD_DIM // 4
ROPE_THETA = 500000.0
QK_NOPE_DIM = 128
QK_ROPE_DIM = 64
V_HEAD_DIM = 128
MLA_QK_PAD = 256
Q_BLOCK = 128
DILATIONS = (1, 4, 16)
NORM_EPS = 1e-6
NEG_INF = -1e30
LOG2E = 1.4426950408889634
VMEM_LIMIT = 56 * 1024 * 1024


def _rms(x, gain):
    ms = jnp.mean(x * x, axis=-1, keepdims=True)
    return x * lax.rsqrt(ms + NORM_EPS) * gain


def _rope_tables(pos, inv_freq, half):
    lane = lax.broadcasted_iota(jnp.int32, (1, LANES), 1)
    ang = pos * inv_freq
    sign = jnp.where(lane < half, -1.0, 1.0).astype(F32)
    return jnp.cos(ang), jnp.sin(ang) * sign


def _rope(r, cos, sin, half):
    lane = lax.broadcasted_iota(jnp.int32, (1, LANES), 1)
    partner = jnp.where(lane < half,
                        pltpu.roll(r, LANES - half, 1),
                        pltpu.roll(r, half, 1))
    return r * cos + partner * sin


ROW_GROUP = 256


def _row_groups(tm):
    assert tm % ROW_GROUP == 0
    return [slice(r, r + ROW_GROUP) for r in range(0, tm, ROW_GROUP)]


def _inproj_kernel(x_ref, pos_ref, g_ref, w_ref, invfa_ref, invfb_ref, gq_ref, gkv_ref,
                   aq_ref, ak_ref, av_ref, cq_ref, ckv_ref, kr_ref, cosb_ref, sinb_ref, *,
                   a_width, q_rank, kv_rank):
    n_heads = a_width // HEAD_DIM
    q_scale = HEAD_DIM ** -0.5 * LOG2E
    tm = x_ref.shape[0]
    for rows in _row_groups(tm):
        h = _rms(x_ref[rows, :], g_ref[...]).astype(BF16)
        pos = pos_ref[rows, :]
        cos_a, sin_a = _rope_tables(pos, invfa_ref[...], ROT_DIM // 2)
        cos_b, sin_b = _rope_tables(pos, invfb_ref[...], QK_ROPE_DIM // 2)
        cosb_ref[rows, :] = cos_b
        sinb_ref[rows, :] = sin_b

        r = jnp.dot(h, w_ref[:, 0:a_width], preferred_element_type=F32)
        for hd in range(n_heads):
            sl = slice(hd * HEAD_DIM, (hd + 1) * HEAD_DIM)
            aq_ref[hd, rows, :] = _rope(r[:, sl], cos_a, sin_a, ROT_DIM // 2) * q_scale
        r = jnp.dot(h, w_ref[:, a_width:2 * a_width], preferred_element_type=F32)
        for hd in range(n_heads):
            sl = slice(hd * HEAD_DIM, (hd + 1) * HEAD_DIM)
            ak_ref[hd, rows, :] = _rope(r[:, sl], cos_a, sin_a, ROT_DIM // 2)
        r = jnp.dot(h, w_ref[:, 2 * a_width:3 * a_width], preferred_element_type=F32)
        for hd in range(n_heads):
            av_ref[hd, rows, :] = r[:, hd * HEAD_DIM:(hd + 1) * HEAD_DIM]

        c0 = 3 * a_width
        c1 = c0 + q_rank + kv_rank
        r = jnp.dot(h, w_ref[:, c0:c1], preferred_element_type=F32)
        cq_ref[rows, :] = _rms(r[:, 0:q_rank], gq_ref[...]).astype(BF16)
        ckv_ref[rows, :] = _rms(r[:, q_rank:q_rank + kv_rank], gkv_ref[...]).astype(BF16)
        kr = jnp.dot(h, w_ref[:, c1:], preferred_element_type=F32)
        kr = jnp.concatenate([kr, jnp.zeros((kr.shape[0], LANES - kr.shape[1]), F32)], axis=1)
        kr_ref[rows, :] = _rope(kr, cos_b, sin_b, QK_ROPE_DIM // 2).astype(BF16)


def _inproj(x2d, pos, g_pre, w_in, invfa, invfb, gq, gkv, *, batch, a_width, q_rank, kv_rank, tm):
    T, D = x2d.shape
    S = T // batch
    assert S % tm == 0
    tiles = S // tm
    n_heads = a_width // HEAD_DIM
    ncols = w_in.shape[1]
    const = lambda i: (0, 0)
    row = lambda i: (i, 0)
    head_major = pl.BlockSpec((None, n_heads, tm, HEAD_DIM),
                              lambda i: (i // tiles, 0, i % tiles, 0))
    head_shape = jax.ShapeDtypeStruct((batch, n_heads, S, HEAD_DIM), F32)
    kern = functools.partial(_inproj_kernel, a_width=a_width, q_rank=q_rank, kv_rank=kv_rank)
    return pl.pallas_call(
        kern,
        grid=(T // tm,),
        in_specs=[
            pl.BlockSpec((tm, D), row),
            pl.BlockSpec((tm, 1), row),
            pl.BlockSpec((1, D), const),
            pl.BlockSpec((D, ncols), const, pipeline_mode=pl.Buffered(1)),
            pl.BlockSpec((1, LANES), const),
            pl.BlockSpec((1, LANES), const),
            pl.BlockSpec((1, q_rank), const),
            pl.BlockSpec((1, kv_rank), const),
        ],
        out_specs=[
            head_major, head_major, head_major,
            pl.BlockSpec((tm, q_rank), row),
            pl.BlockSpec((tm, kv_rank), row),
            pl.BlockSpec((tm, LANES), row),
            pl.BlockSpec((tm, LANES), row),
            pl.BlockSpec((tm, LANES), row),
        ],
        out_shape=[
            head_shape, head_shape, head_shape,
            jax.ShapeDtypeStruct((T, q_rank), BF16),
            jax.ShapeDtypeStruct((T, kv_rank), BF16),
            jax.ShapeDtypeStruct((T, LANES), BF16),
            jax.ShapeDtypeStruct((T, LANES), F32),
            jax.ShapeDtypeStruct((T, LANES), F32),
        ],
        compiler_params=pltpu.CompilerParams(
            dimension_semantics=("parallel",), vmem_limit_bytes=VMEM_LIMIT),
        name="inproj",
    )(x2d, pos, g_pre, w_in, invfa, invfb, gq, gkv)


def _mla_up_kernel(cq_ref, ckv_ref, kr_ref, cosb_ref, sinb_ref, wq_ref, wkv_ref,
                   q_ref, k_ref, v_ref, *, n_heads):
    scale = (QK_NOPE_DIM + QK_ROPE_DIM) ** -0.5 * LOG2E
    tm = cq_ref.shape[0]
    for rows in _row_groups(tm):
        cos_b, sin_b = cosb_ref[rows, :], sinb_ref[rows, :]
        rq = jnp.dot(cq_ref[rows, :], wq_ref[...], preferred_element_type=F32)
        rkv = jnp.dot(ckv_ref[rows, :], wkv_ref[...], preferred_element_type=F32)
        kr = kr_ref[rows, :]
        rope_pad = jnp.zeros((rq.shape[0], LANES - QK_ROPE_DIM), F32)
        for hd in range(n_heads):
            base = hd * (QK_NOPE_DIM + QK_ROPE_DIM)
            q_ref[hd, rows, 0:LANES] = (rq[:, base:base + QK_NOPE_DIM] * scale).astype(BF16)
            q_rope = jnp.concatenate(
                [rq[:, base + QK_NOPE_DIM:base + QK_NOPE_DIM + QK_ROPE_DIM], rope_pad], axis=1)
            q_rot = _rope(q_rope, cos_b, sin_b, QK_ROPE_DIM // 2)
            q_ref[hd, rows, LANES:2 * LANES] = (q_rot * scale).astype(BF16)
            kv0 = hd * (QK_NOPE_DIM + V_HEAD_DIM)
            k_ref[hd, rows, 0:LANES] = rkv[:, kv0:kv0 + QK_NOPE_DIM].astype(BF16)
            k_ref[hd, rows, LANES:2 * LANES] = kr
            v_ref[hd, rows, :] = rkv[:, kv0 + QK_NOPE_DIM:kv0 + QK_NOPE_DIM + V_HEAD_DIM].astype(BF16)


def _mla_up(cq, ckv, kr, cos_b, sin_b, wq, wkv, *, batch, n_heads, tm):
    T = cq.shape[0]
    S = T // batch
    assert S % tm == 0
    tiles = S // tm
    const = lambda i: (0, 0)
    row = lambda i: (i, 0)
    head_major = lambda width: pl.BlockSpec((None, n_heads, tm, width),
                                            lambda i: (i // tiles, 0, i % tiles, 0))
    head_shape = lambda width: jax.ShapeDtypeStruct((batch, n_heads, S, width), BF16)
    return pl.pallas_call(
        functools.partial(_mla_up_kernel, n_heads=n_heads),
        grid=(T // tm,),
        in_specs=[
            pl.BlockSpec((tm, cq.shape[1]), row),
            pl.BlockSpec((tm, ckv.shape[1]), row),
            pl.BlockSpec((tm, LANES), row),
            pl.BlockSpec((tm, LANES), row),
            pl.BlockSpec((tm, LANES), row),
            pl.BlockSpec(wq.shape, const),
            pl.BlockSpec(wkv.shape, const),
        ],
        out_specs=[head_major(MLA_QK_PAD), head_major(MLA_QK_PAD), head_major(V_HEAD_DIM)],
        out_shape=[head_shape(MLA_QK_PAD), head_shape(MLA_QK_PAD), head_shape(V_HEAD_DIM)],
        compiler_params=pltpu.CompilerParams(
            dimension_semantics=("parallel",), vmem_limit_bytes=VMEM_LIMIT),
        name="mla_up",
    )(cq, ckv, kr, cos_b, sin_b, wq, wkv)


MAX_DIL = 16
NT_DIMS = (((1,), (1,)), ((), ()))


def _softmax_block(q, k, v, bias, prev):
    nk = k.shape[0]
    s = lax.dot_general(q, k, NT_DIMS, preferred_element_type=F32)
    blocks = [s[:, i * LANES:(i + 1) * LANES] for i in range(nk // LANES)]
    if bias is not None:
        nb = bias.shape[1] // LANES
        blocks[-nb:] = [b + bias[:, i * LANES:(i + 1) * LANES]
                        for i, b in enumerate(blocks[-nb:])]
    m_cur = jnp.max(functools.reduce(jnp.maximum, blocks), axis=1, keepdims=True)
    if prev is None:
        m_new = jnp.broadcast_to(m_cur, (q.shape[0], LANES))
    else:
        m_prev, l_prev, acc_prev = prev
        m_new = jnp.maximum(m_prev, m_cur)
        alpha = jnp.exp2(m_prev - m_new)
    p = jnp.concatenate([jnp.exp2(b - m_new) for b in blocks], axis=1).astype(BF16)
    v_aug = jnp.concatenate([v, jnp.ones((nk, LANES), BF16)], axis=1)
    pv = jnp.dot(p, v_aug, preferred_element_type=F32)
    acc_cur, l_cur = pv[:, :LANES], pv[:, LANES:]
    if prev is None:
        return m_new, l_cur, acc_cur
    return m_new, alpha * l_prev + l_cur, alpha * acc_prev + acc_cur


def _dswa_kernel(q_ref, k_ref, v_ref, o_ref, xq, xk, xv, mx, lx, ax, mn, ln, an, x4, *, seq):
    U = seq // MAX_DIL
    QB = Q_BLOCK
    NEG = NEG_INF

    def band_bias(dist):
        return jnp.where(dist >= 0, jnp.where(dist <= QB, 0.0, NEG), NEG).astype(F32)

    i1 = lax.broadcasted_iota(jnp.int32, (QB, QB), 0)
    j1 = lax.broadcasted_iota(jnp.int32, (QB, QB), 1)
    i2 = lax.broadcasted_iota(jnp.int32, (QB, 2 * QB), 0)
    j2 = lax.broadcasted_iota(jnp.int32, (QB, 2 * QB), 1)
    bias_first = band_bias(i1 - j1)
    bias_band = band_bias(i2 + QB - j2)
    qc = QB // 4
    lo = lambda x, n: x & (n - 1)
    hi = lambda x, n: x >> (n.bit_length() - 1)
    d4_first = band_bias(4 * (lo(i1, qc) - lo(j1, qc)) + hi(i1, qc) - hi(j1, qc))
    d4_band = band_bias(4 * (lo(i2, qc) - lo(j2, 2 * qc) + qc) + hi(i2, qc) - hi(j2, 2 * qc))

    for src, dst in ((q_ref, xq), (k_ref, xk), (v_ref, xv)):
        for r4 in range(4):
            x4[r4] = src[pl.ds(r4, 4 * U, stride=4), :]
        for r in range(MAX_DIL):
            dst[r] = x4[r % 4, pl.ds(r // 4, U, stride=4), :].astype(BF16)

    for r in range(MAX_DIL):
        for n in range(U // QB):
            qs = slice(n * QB, (n + 1) * QB)
            ks = qs if n == 0 else slice((n - 1) * QB, (n + 1) * QB)
            res = _softmax_block(xq[r, qs, :], xk[r, ks, :], xv[r, ks, :],
                                 bias_first if n == 0 else bias_band, None)
            for ref, val in zip((mx, lx, ax), res):
                ref[r, qs, :] = val

    def d4_tile(r4, a, first):
        def gather(ref, start, size):
            return jnp.concatenate([ref[4 * qq + r4, start:start + size, :] for qq in range(4)],
                                   axis=0)
        q = gather(xq, a, qc)
        prev = (gather(mx, a, qc), gather(lx, a, qc), gather(ax, a, qc))
        if first:
            k, v, bias = gather(xk, a, qc), gather(xv, a, qc), d4_first
        else:
            k, v, bias = gather(xk, a - qc, 2 * qc), gather(xv, a - qc, 2 * qc), d4_band
        return _softmax_block(q, k, v, bias, prev)

    for a in range(0, U, qc):
        for r4 in range(4):
            res = d4_tile(r4, a, a == 0)
            for ref, val in zip((mx, lx, ax), res):
                for qq in range(4):
                    ref[4 * qq + r4, a:a + qc, :] = val[qq * qc:(qq + 1) * qc]

    for src, dst in ((mx, mn), (lx, ln), (ax, an)):
        for r in range(MAX_DIL):
            x4[r % 4, pl.ds(r // 4, U, stride=4), :] = src[r]
        for r4 in range(4):
            dst[pl.ds(r4, 4 * U, stride=4), :] = x4[r4]

    for n in range(seq // QB):
        qs = slice(n * QB, (n + 1) * QB)
        ks = qs if n == 0 else slice((n - 1) * QB, (n + 1) * QB)
        prev = (mn[qs, :], ln[qs, :], an[qs, :])
        _, l_new, acc_new = _softmax_block(
            q_ref[qs, :].astype(BF16), k_ref[ks, :].astype(BF16), v_ref[ks, :].astype(BF16),
            bias_first if n == 0 else bias_band, prev)
        o_ref[qs, :] = (acc_new / l_new).astype(o_ref.dtype)


def _dswa(aq, ak, av):
    B, n_heads, S, _ = aq.shape
    U = S // MAX_DIL
    assert DILATIONS == (1, 4, MAX_DIL) and U % Q_BLOCK == 0
    spec = pl.BlockSpec((None, None, S, HEAD_DIM), lambda b, h: (b, h, 0, 0))
    x16 = lambda dt: pltpu.VMEM((MAX_DIL, U, LANES), dt)
    return pl.pallas_call(
        functools.partial(_dswa_kernel, seq=S),
        grid=(B, n_heads),
        in_specs=[spec, spec, spec],
        out_specs=pl.BlockSpec((None, S, HEAD_DIM), lambda b, h: (b, 0, h)),
        out_shape=jax.ShapeDtypeStruct((B, S, n_heads * HEAD_DIM), BF16),
        scratch_shapes=([x16(BF16)] * 3 + [x16(F32)] * 3 + [pltpu.VMEM((S, LANES), F32)] * 3
                        + [pltpu.VMEM((4, S // 4, LANES), F32)]),
        compiler_params=pltpu.CompilerParams(
            dimension_semantics=("parallel", "parallel"), vmem_limit_bytes=VMEM_LIMIT),
        name="dswa",
    )(aq, ak, av)


MLA_CHAIN = 256


def _cast_specs(weights, n_steps, index):
    specs, shapes = [], []
    for w in weights:
        rows = w.shape[0] // n_steps
        assert rows * n_steps == w.shape[0] and rows % 16 == 0
        specs.append(pl.BlockSpec((rows, w.shape[1]), index))
        shapes.append(jax.ShapeDtypeStruct(w.shape, BF16))
    return specs, shapes


def _mla_attn_kernel(q_ref, k_ref, v_ref, *rest, seq, tk):
    n_cast = (len(rest) - 1) // 2
    o_ref = rest[n_cast]
    for src, dst in zip(rest[:n_cast], rest[n_cast + 1:]):
        dst[...] = src[...].astype(BF16)
    _mla_attn_body(q_ref, k_ref, v_ref, o_ref, seq=seq, tk=tk)


def _mla_attn_body(q_ref, k_ref, v_ref, o_ref, *, seq, tk):
    cq = MLA_CHAIN
    n_chains = seq // cq
    ri = lax.broadcasted_iota(jnp.int32, (cq, cq), 0)
    ci = lax.broadcasted_iota(jnp.int32, (cq, cq), 1)
    diag_bias = jnp.where(ci <= ri, 0.0, NEG_INF).astype(F32)

    plans = []
    for c in range(n_chains):
        n_full = (c * cq) // tk
        plan = [(j * tk, tk, None) for j in range(n_full)]
        plan.append((n_full * tk, (c + 1) * cq - n_full * tk, diag_bias))
        plans.append(plan)

    state = [None] * n_chains
    for t in range(max(len(p) for p in plans)):
        for c in reversed(range(n_chains)):
            if t >= len(plans[c]):
                continue
            start, nk, bias = plans[c][t]
            rows = slice(c * cq, (c + 1) * cq)
            ks = slice(start, start + nk)
            state[c] = _softmax_block(q_ref[rows, :], k_ref[ks, :], v_ref[ks, :], bias, state[c])
            if t == len(plans[c]) - 1:
                _, l, acc = state[c]
                o_ref[rows, :] = (acc / l).astype(o_ref.dtype)


def _mla_attn(q, k, v, cast_weights, *, tk):
    B, n_heads, S, _ = q.shape
    assert tk % MLA_CHAIN == 0 and S % tk == 0
    head = lambda width: pl.BlockSpec((None, None, S, width), lambda b, h: (b, h, 0, 0))
    cast_specs, cast_shapes = _cast_specs(cast_weights, B * n_heads,
                                          lambda b, h: (b * n_heads + h, 0))
    return pl.pallas_call(
        functools.partial(_mla_attn_kernel, seq=S, tk=tk),
        grid=(B, n_heads),
        in_specs=[head(MLA_QK_PAD), head(MLA_QK_PAD), head(V_HEAD_DIM)] + cast_specs,
        out_specs=[pl.BlockSpec((None, S, V_HEAD_DIM), lambda b, h: (b, 0, h))] + cast_specs,
        out_shape=[jax.ShapeDtypeStruct((B, S, n_heads * V_HEAD_DIM), BF16)] + cast_shapes,
        compiler_params=pltpu.CompilerParams(
            dimension_semantics=("parallel", "parallel"), vmem_limit_bytes=VMEM_LIMIT),
        name="mla_attn",
    )(q, k, v, *cast_weights)


def _outproj_kernel(a_ref, b_ref, x_ref, w_ref, g_ref, o_ref, *, a_width):
    tm = x_ref.shape[0]
    for rows in _row_groups(tm):
        y = jnp.dot(a_ref[rows, :], w_ref[0:a_width, :], preferred_element_type=F32)
        y = y + jnp.dot(b_ref[rows, :], w_ref[a_width:, :], preferred_element_type=F32)
        o_ref[rows, :] = x_ref[rows, :] + _rms(y, g_ref[...])


def _outproj(a, b, x2d, w_out, g_post, *, tm):
    T, D = x2d.shape
    const = lambda i: (0, 0)
    row = lambda i: (i, 0)
    return pl.pallas_call(
        functools.partial(_outproj_kernel, a_width=a.shape[1]),
        grid=(T // tm,),
        in_specs=[
            pl.BlockSpec((tm, a.shape[1]), row),
            pl.BlockSpec((tm, b.shape[1]), row),
            pl.BlockSpec((tm, D), row),
            pl.BlockSpec(w_out.shape, const, pipeline_mode=pl.Buffered(1)),
            pl.BlockSpec((1, D), const),
        ],
        out_specs=pl.BlockSpec((tm, D), row),
        out_shape=jax.ShapeDtypeStruct((T, D), F32),
        compiler_params=pltpu.CompilerParams(
            dimension_semantics=("parallel",), vmem_limit_bytes=VMEM_LIMIT),
        name="outproj",
    )(a, b, x2d, w_out, g_post)


def _mlp_kernel(x_ref, gpre_ref, gpost_ref, wu_ref, wd_ref, o_ref, h_sc):
    f = pl.program_id(1)
    last = pl.num_programs(1) - 1
    tm = x_ref.shape[0]

    def step(first, final):
        for rows in _row_groups(tm):
            if first:
                h = _rms(x_ref[rows, :], gpre_ref[...]).astype(BF16)
                h_sc[rows, :] = h
            else:
                h = h_sc[rows, :]
            u = jnp.dot(h, wu_ref[...], preferred_element_type=F32)
            u = jnp.square(jnp.maximum(u, 0.0)).astype(BF16)
            y = jnp.dot(u, wd_ref[...], preferred_element_type=F32)
            acc = y if first else o_ref[rows, :] + y
            if final:
                o_ref[rows, :] = x_ref[rows, :] + _rms(acc, gpost_ref[...])
            else:
                o_ref[rows, :] = acc

    pl.when(f == 0)(lambda: step(True, False))
    pl.when(jnp.logical_and(f > 0, f < last))(lambda: step(False, False))
    pl.when(f == last)(lambda: step(False, True))


def _mlp(x2d, g_pre, g_post, w_up, w_down, *, tm, tf):
    T, D = x2d.shape
    F = w_up.shape[1]
    return pl.pallas_call(
        _mlp_kernel,
        grid=(T // tm, F // tf),
        in_specs=[
            pl.BlockSpec((tm, D), lambda i, f: (i, 0)),
            pl.BlockSpec((1, D), lambda i, f: (0, 0)),
            pl.BlockSpec((1, D), lambda i, f: (0, 0)),
            pl.BlockSpec((D, tf), lambda i, f: (0, f)),
            pl.BlockSpec((tf, D), lambda i, f: (f, 0)),
        ],
        out_specs=pl.BlockSpec((tm, D), lambda i, f: (i, 0)),
        out_shape=jax.ShapeDtypeStruct((T, D), F32),
        scratch_shapes=[pltpu.VMEM((tm, D), BF16)],
        compiler_params=pltpu.CompilerParams(
            dimension_semantics=("parallel", "arbitrary"), vmem_limit_bytes=VMEM_LIMIT),
        name="mlp",
    )(x2d, g_pre, g_post, w_up, w_down)


def _inv_freq_lanes(rot_dim):
    inv = ROPE_THETA ** (-jnp.arange(0, rot_dim, 2, dtype=F32) / rot_dim)
    lanes = jnp.concatenate([inv, inv, jnp.zeros((LANES - rot_dim,), F32)])
    return lanes.reshape(1, LANES)


def _layer(x2d, pos, B, S, norm_attn_pre, norm_attn_post, w_in, q_latent_norm, kv_latent_norm,
           w_uq, w_ukv, w_out, norm_mlp_pre, norm_mlp_post, w_up, w_down):
    D = x2d.shape[1]
    q_rank = q_latent_norm.shape[0]
    kv_rank = kv_latent_norm.shape[0]
    n_mla = w_uq.shape[1] // (QK_NOPE_DIM + QK_ROPE_DIM)
    a_width = (w_in.shape[1] - q_rank - kv_rank - QK_ROPE_DIM) // 3

    w_in_p = w_in.astype(BF16)
    wq_p = w_uq.astype(BF16)
    wkv_p = w_ukv.astype(BF16)

    invfa = _inv_freq_lanes(ROT_DIM)
    invfb = _inv_freq_lanes(QK_ROPE_DIM)
    row = lambda g: g.reshape(1, -1)

    aq, ak, av, cq, ckv, kr, cos_b, sin_b = _inproj(
        x2d, pos, row(norm_attn_pre), w_in_p, invfa, invfb, row(q_latent_norm),
        row(kv_latent_norm), batch=B, a_width=a_width, q_rank=q_rank, kv_rank=kv_rank, tm=512)
    q_b, k_b, v_b = _mla_up(cq, ckv, kr, cos_b, sin_b, wq_p, wkv_p, batch=B, n_heads=n_mla,
                            tm=1024)

    a_out = _dswa(aq, ak, av)
    b_out, w_out_b, w_up_b, w_down_b = _mla_attn(q_b, k_b, v_b, [w_out, w_up, w_down], tk=512)

    x2d = _outproj(a_out.reshape(B * S, -1), b_out.reshape(B * S, -1), x2d,
                   w_out_b, row(norm_attn_post), tm=1024)
    return _mlp(x2d, row(norm_mlp_pre), row(norm_mlp_post), w_up_b, w_down_b, tm=512, tf=2048)


def kernel(x, positions, norm_attn_pre, norm_attn_post, w_in, q_latent_norm, kv_latent_norm,
           w_uq, w_ukv, w_out, norm_mlp_pre, norm_mlp_post, w_up, w_down):
    B, S, D = x.shape
    x2d = x.reshape(B * S, D)
    pos = positions.astype(F32).reshape(B * S, 1)
    for layer in range(w_in.shape[0]):
        x2d = _layer(x2d, pos, B, S, norm_attn_pre[layer], norm_attn_post[layer], w_in[layer],
                     q_latent_norm[layer], kv_latent_norm[layer], w_uq[layer], w_ukv[layer],
                     w_out[layer], norm_mlp_pre[layer], norm_mlp_post[layer], w_up[layer],
                     w_down[layer])
    return x2d.reshape(B, S, D)
```

```python
import functools

import jax
import jax.numpy as jnp
from jax import lax
from jax.experimental import pallas as pl
from jax.experimental.pallas import tpu as pltpu

F32 = jnp.float32
BF16 = jnp.bfloat16

LANES = 128
HEAD_DIM = 128
ROT_DIM = HEAD_DIM // 4
ROPE_THETA = 500000.0
QK_NOPE_DIM = 128
QK_ROPE_DIM = 64
V_HEAD_DIM = 128
MLA_QK_PAD = 256
Q_BLOCK = 128
DILATIONS = (1, 4, 16)
NORM_EPS = 1e-6
NEG_INF = -1e30
LOG2E = 1.4426950408889634
VMEM_LIMIT = 56 * 1024 * 1024


def _rms(x, gain):
    ms = jnp.mean(x * x, axis=-1, keepdims=True)
    return x * lax.rsqrt(ms + NORM_EPS) * gain


def _rope_tables(pos, inv_freq, half):
    lane = lax.broadcasted_iota(jnp.int32, (1, LANES), 1)
    ang = pos * inv_freq
    sign = jnp.where(lane < half, -1.0, 1.0).astype(F32)
    return jnp.cos(ang), jnp.sin(ang) * sign


def _rope(r, cos, sin, half):
    lane = lax.broadcasted_iota(jnp.int32, (1, LANES), 1)
    partner = jnp.where(lane < half,
                        pltpu.roll(r, LANES - half, 1),
                        pltpu.roll(r, half, 1))
    return r * cos + partner * sin


ROW_GROUP = 256


def _row_groups(tm):
    assert tm % ROW_GROUP == 0
    return [slice(r, r + ROW_GROUP) for r in range(0, tm, ROW_GROUP)]


def _inproj_kernel(x_ref, pos_ref, g_ref, w_ref, invfa_ref, invfb_ref, gq_ref, gkv_ref,
                   aq_ref, ak_ref, av_ref, cq_ref, ckv_ref, kr_ref, cosb_ref, sinb_ref, *,
                   a_width, q_rank, kv_rank):
    n_heads = a_width // HEAD_DIM
    q_scale = HEAD_DIM ** -0.5 * LOG2E
    tm = x_ref.shape[0]
    for rows in _row_groups(tm):
        h = _rms(x_ref[rows, :], g_ref[...]).astype(BF16)
        pos = pos_ref[rows, :]
        cos_a, sin_a = _rope_tables(pos, invfa_ref[...], ROT_DIM // 2)
        cos_b, sin_b = _rope_tables(pos, invfb_ref[...], QK_ROPE_DIM // 2)
        cosb_ref[rows, :] = cos_b
        sinb_ref[rows, :] = sin_b

        r = jnp.dot(h, w_ref[:, 0:a_width], preferred_element_type=F32)
        for hd in range(n_heads):
            sl = slice(hd * HEAD_DIM, (hd + 1) * HEAD_DIM)
            aq_ref[hd, rows, :] = _rope(r[:, sl], cos_a, sin_a, ROT_DIM // 2) * q_scale
        r = jnp.dot(h, w_ref[:, a_width:2 * a_width], preferred_element_type=F32)
        for hd in range(n_heads):
            sl = slice(hd * HEAD_DIM, (hd + 1) * HEAD_DIM)
            ak_ref[hd, rows, :] = _rope(r[:, sl], cos_a, sin_a, ROT_DIM // 2)
        r = jnp.dot(h, w_ref[:, 2 * a_width:3 * a_width], preferred_element_type=F32)
        for hd in range(n_heads):
            av_ref[hd, rows, :] = r[:, hd * HEAD_DIM:(hd + 1) * HEAD_DIM]

        c0 = 3 * a_width
        c1 = c0 + q_rank + kv_rank
        r = jnp.dot(h, w_ref[:, c0:c1], preferred_element_type=F32)
        cq_ref[rows, :] = _rms(r[:, 0:q_rank], gq_ref[...]).astype(BF16)
        ckv_ref[rows, :] = _rms(r[:, q_rank:q_rank + kv_rank], gkv_ref[...]).astype(BF16)
        kr = jnp.dot(h, w_ref[:, c1:], preferred_element_type=F32)
        kr = jnp.concatenate([kr, jnp.zeros((kr.shape[0], LANES - kr.shape[1]), F32)], axis=1)
        kr_ref[rows, :] = _rope(kr, cos_b, sin_b, QK_ROPE_DIM // 2).astype(BF16)


def _inproj(x2d, pos, g_pre, w_in, invfa, invfb, gq, gkv, *, batch, a_width, q_rank, kv_rank, tm):
    T, D = x2d.shape
    S = T // batch
    assert S % tm == 0
    tiles = S // tm
    n_heads = a_width // HEAD_DIM
    ncols = w_in.shape[1]
    const = lambda i: (0, 0)
    row = lambda i: (i, 0)
    head_major = pl.BlockSpec((None, n_heads, tm, HEAD_DIM),
                              lambda i: (i // tiles, 0, i % tiles, 0))
    head_shape = jax.ShapeDtypeStruct((batch, n_heads, S, HEAD_DIM), F32)
    kern = functools.partial(_inproj_kernel, a_width=a_width, q_rank=q_rank, kv_rank=kv_rank)
    return pl.pallas_call(
        kern,
        grid=(T // tm,),
        in_specs=[
            pl.BlockSpec((tm, D), row),
            pl.BlockSpec((tm, 1), row),
            pl.BlockSpec((1, D), const),
            pl.BlockSpec((D, ncols), const, pipeline_mode=pl.Buffered(1)),
            pl.BlockSpec((1, LANES), const),
            pl.BlockSpec((1, LANES), const),
            pl.BlockSpec((1, q_rank), const),
            pl.BlockSpec((1, kv_rank), const),
        ],
        out_specs=[
            head_major, head_major, head_major,
            pl.BlockSpec((tm, q_rank), row),
            pl.BlockSpec((tm, kv_rank), row),
            pl.BlockSpec((tm, LANES), row),
            pl.BlockSpec((tm, LANES), row),
            pl.BlockSpec((tm, LANES), row),
        ],
        out_shape=[
            head_shape, head_shape, head_shape,
            jax.ShapeDtypeStruct((T, q_rank), BF16),
            jax.ShapeDtypeStruct((T, kv_rank), BF16),
            jax.ShapeDtypeStruct((T, LANES), BF16),
            jax.ShapeDtypeStruct((T, LANES), F32),
            jax.ShapeDtypeStruct((T, LANES), F32),
        ],
        compiler_params=pltpu.CompilerParams(
            dimension_semantics=("parallel",), vmem_limit_bytes=VMEM_LIMIT),
        name="inproj",
    )(x2d, pos, g_pre, w_in, invfa, invfb, gq, gkv)


def _mla_up_kernel(cq_ref, ckv_ref, kr_ref, cosb_ref, sinb_ref, wq_ref, wkv_ref,
                   q_ref, k_ref, v_ref, *, n_heads):
    scale = (QK_NOPE_DIM + QK_ROPE_DIM) ** -0.5 * LOG2E
    tm = cq_ref.shape[0]
    for rows in _row_groups(tm):
        cos_b, sin_b = cosb_ref[rows, :], sinb_ref[rows, :]
        rq = jnp.dot(cq_ref[rows, :], wq_ref[...], preferred_element_type=F32)
        rkv = jnp.dot(ckv_ref[rows, :], wkv_ref[...], preferred_element_type=F32)
        kr = kr_ref[rows, :]
        rope_pad = jnp.zeros((rq.shape[0], LANES - QK_ROPE_DIM), F32)
        for hd in range(n_heads):
            base = hd * (QK_NOPE_DIM + QK_ROPE_DIM)
            q_ref[hd, rows, 0:LANES] = (rq[:, base:base + QK_NOPE_DIM] * scale).astype(BF16)
            q_rope = jnp.concatenate(
                [rq[:, base + QK_NOPE_DIM:base + QK_NOPE_DIM + QK_ROPE_DIM], rope_pad], axis=1)
            q_rot = _rope(q_rope, cos_b, sin_b, QK_ROPE_DIM // 2)
            q_ref[hd, rows, LANES:2 * LANES] = (q_rot * scale).astype(BF16)
            kv0 = hd * (QK_NOPE_DIM + V_HEAD_DIM)
            k_ref[hd, rows, 0:LANES] = rkv[:, kv0:kv0 + QK_NOPE_DIM].astype(BF16)
            k_ref[hd, rows, LANES:2 * LANES] = kr
            v_ref[hd, rows, :] = rkv[:, kv0 + QK_NOPE_DIM:kv0 + QK_NOPE_DIM + V_HEAD_DIM].astype(BF16)


def _mla_up(cq, ckv, kr, cos_b, sin_b, wq, wkv, *, batch, n_heads, tm):
    T = cq.shape[0]
    S = T // batch
    assert S % tm == 0
    tiles = S // tm
    const = lambda i: (0, 0)
    row = lambda i: (i, 0)
    head_major = lambda width: pl.BlockSpec((None, n_heads, tm, width),
                                            lambda i: (i // tiles, 0, i % tiles, 0))
    head_shape = lambda width: jax.ShapeDtypeStruct((batch, n_heads, S, width), BF16)
    return pl.pallas_call(
        functools.partial(_mla_up_kernel, n_heads=n_heads),
        grid=(T // tm,),
        in_specs=[
            pl.BlockSpec((tm, cq.shape[1]), row),
            pl.BlockSpec((tm, ckv.shape[1]), row),
            pl.BlockSpec((tm, LANES), row),
            pl.BlockSpec((tm, LANES), row),
            pl.BlockSpec((tm, LANES), row),
            pl.BlockSpec(wq.shape, const),
            pl.BlockSpec(wkv.shape, const),
        ],
        out_specs=[head_major(MLA_QK_PAD), head_major(MLA_QK_PAD), head_major(V_HEAD_DIM)],
        out_shape=[head_shape(MLA_QK_PAD), head_shape(MLA_QK_PAD), head_shape(V_HEAD_DIM)],
        compiler_params=pltpu.CompilerParams(
            dimension_semantics=("parallel",), vmem_limit_bytes=VMEM_LIMIT),
        name="mla_up",
    )(cq, ckv, kr, cos_b, sin_b, wq, wkv)


MAX_DIL = 16
NT_DIMS = (((1,), (1,)), ((), ()))


def _softmax_block(q, k, v, bias, prev):
    nk = k.shape[0]
    s = lax.dot_general(q, k, NT_DIMS, preferred_element_type=F32)
    blocks = [s[:, i * LANES:(i + 1) * LANES] for i in range(nk // LANES)]
    if bias is not None:
        nb = bias.shape[1] // LANES
        blocks[-nb:] = [b + bias[:, i * LANES:(i + 1) * LANES]
                        for i, b in enumerate(blocks[-nb:])]
    m_cur = jnp.max(functools.reduce(jnp.maximum, blocks), axis=1, keepdims=True)
    if prev is None:
        m_new = jnp.broadcast_to(m_cur, (q.shape[0], LANES))
    else:
        m_prev, l_prev, acc_prev = prev
        m_new = jnp.maximum(m_prev, m_cur)
        alpha = jnp.exp2(m_prev - m_new)
    p = jnp.concatenate([jnp.exp2(b - m_new) for b in blocks], axis=1).astype(BF16)
    v_aug = jnp.concatenate([v, jnp.ones((nk, LANES), BF16)], axis=1)
    pv = jnp.dot(p, v_aug, preferred_element_type=F32)
    acc_cur, l_cur = pv[:, :LANES], pv[:, LANES:]
    if prev is None:
        return m_new, l_cur, acc_cur
    return m_new, alpha * l_prev + l_cur, alpha * acc_prev + acc_cur


def _dswa_kernel(q_ref, k_ref, v_ref, o_ref, xq, xk, xv, mx, lx, ax, mn, ln, an, x4, *, seq):
    U = seq // MAX_DIL
    QB = Q_BLOCK
    NEG = NEG_INF

    def band_bias(dist):
        return jnp.where(dist >= 0, jnp.where(dist <= QB, 0.0, NEG), NEG).astype(F32)

    i1 = lax.broadcasted_iota(jnp.int32, (QB, QB), 0)
    j1 = lax.broadcasted_iota(jnp.int32, (QB, QB), 1)
    i2 = lax.broadcasted_iota(jnp.int32, (QB, 2 * QB), 0)
    j2 = lax.broadcasted_iota(jnp.int32, (QB, 2 * QB), 1)
    bias_first = band_bias(i1 - j1)
    bias_band = band_bias(i2 + QB - j2)
    qc = QB // 4
    lo = lambda x, n: x & (n - 1)
    hi = lambda x, n: x >> (n.bit_length() - 1)
    d4_first = band_bias(4 * (lo(i1, qc) - lo(j1, qc)) + hi(i1, qc) - hi(j1, qc))
    d4_band = band_bias(4 * (lo(i2, qc) - lo(j2, 2 * qc) + qc) + hi(i2, qc) - hi(j2, 2 * qc))

    for src, dst in ((q_ref, xq), (k_ref, xk), (v_ref, xv)):
        for r4 in range(4):
            x4[r4] = src[pl.ds(r4, 4 * U, stride=4), :]
        for r in range(MAX_DIL):
            dst[r] = x4[r % 4, pl.ds(r // 4, U, stride=4), :].astype(BF16)

    for r in range(MAX_DIL):
        for n in range(U // QB):
            qs = slice(n * QB, (n + 1) * QB)
            ks = qs if n == 0 else slice((n - 1) * QB, (n + 1) * QB)
            res = _softmax_block(xq[r, qs, :], xk[r, ks, :], xv[r, ks, :],
                                 bias_first if n == 0 else bias_band, None)
            for ref, val in zip((mx, lx, ax), res):
                ref[r, qs, :] = val

    def d4_tile(r4, a, first):
        def gather(ref, start, size):
            return jnp.concatenate([ref[4 * qq + r4, start:start + size, :] for qq in range(4)],
                                   axis=0)
        q = gather(xq, a, qc)
        prev = (gather(mx, a, qc), gather(lx, a, qc), gather(ax, a, qc))
        if first:
            k, v, bias = gather(xk, a, qc), gather(xv, a, qc), d4_first
        else:
            k, v, bias = gather(xk, a - qc, 2 * qc), gather(xv, a - qc, 2 * qc), d4_band
        return _softmax_block(q, k, v, bias, prev)

    for a in range(0, U, qc):
        for r4 in range(4):
            res = d4_tile(r4, a, a == 0)
            for ref, val in zip((mx, lx, ax), res):
                for qq in range(4):
                    ref[4 * qq + r4, a:a + qc, :] = val[qq * qc:(qq + 1) * qc]

    for src, dst in ((mx, mn), (lx, ln), (ax, an)):
        for r in range(MAX_DIL):
            x4[r % 4, pl.ds(r // 4, U, stride=4), :] = src[r]
        for r4 in range(4):
            dst[pl.ds(r4, 4 * U, stride=4), :] = x4[r4]

    for n in range(seq // QB):
        qs = slice(n * QB, (n + 1) * QB)
        ks = qs if n == 0 else slice((n - 1) * QB, (n + 1) * QB)
        prev = (mn[qs, :], ln[qs, :], an[qs, :])
        _, l_new, acc_new = _softmax_block(
            q_ref[qs, :].astype(BF16), k_ref[ks, :].astype(BF16), v_ref[ks, :].astype(BF16),
            bias_first if n == 0 else bias_band, prev)
        o_ref[qs, :] = (acc_new / l_new).astype(o_ref.dtype)


DSWA_HEADS = 2


def _dswa_heads_kernel(q_ref, k_ref, v_ref, o_ref, *scratch, seq):
    for hh in range(DSWA_HEADS):
        _dswa_kernel(q_ref.at[hh], k_ref.at[hh], v_ref.at[hh],
                     o_ref.at[:, hh * HEAD_DIM:(hh + 1) * HEAD_DIM], *scratch, seq=seq)


def _dswa(aq, ak, av):
    B, n_heads, S, _ = aq.shape
    U = S // MAX_DIL
    assert DILATIONS == (1, 4, MAX_DIL) and U % Q_BLOCK == 0 and n_heads % DSWA_HEADS == 0
    spec = pl.BlockSpec((None, DSWA_HEADS, S, HEAD_DIM), lambda b, h: (b, h, 0, 0))
    x16 = lambda dt: pltpu.VMEM((MAX_DIL, U, LANES), dt)
    return pl.pallas_call(
        functools.partial(_dswa_heads_kernel, seq=S),
        grid=(B, n_heads // DSWA_HEADS),
        in_specs=[spec, spec, spec],
        out_specs=pl.BlockSpec((None, S, DSWA_HEADS * HEAD_DIM), lambda b, h: (b, 0, h)),
        out_shape=jax.ShapeDtypeStruct((B, S, n_heads * HEAD_DIM), BF16),
        scratch_shapes=([x16(BF16)] * 3 + [x16(F32)] * 3 + [pltpu.VMEM((S, LANES), F32)] * 3
                        + [pltpu.VMEM((4, S // 4, LANES), F32)]),
        compiler_params=pltpu.CompilerParams(
            dimension_semantics=("parallel", "parallel"), vmem_limit_bytes=VMEM_LIMIT),
        name="dswa",
    )(aq, ak, av)


MLA_CHAIN = 256


def _cast_specs(weights, n_steps, index):
    specs, shapes = [], []
    for w in weights:
        rows = w.shape[0] // n_steps
        assert rows * n_steps == w.shape[0] and rows % 16 == 0
        specs.append(pl.BlockSpec((rows, w.shape[1]), index))
        shapes.append(jax.ShapeDtypeStruct(w.shape, BF16))
    return specs, shapes


def _mla_attn_kernel(q_ref, k_ref, v_ref, *rest, seq, tk):
    n_cast = (len(rest) - 1) // 2
    o_ref = rest[n_cast]
    for src, dst in zip(rest[:n_cast], rest[n_cast + 1:]):
        dst[...] = src[...].astype(BF16)
    _mla_attn_body(q_ref, k_ref, v_ref, o_ref, seq=seq, tk=tk)


def _mla_attn_body(q_ref, k_ref, v_ref, o_ref, *, seq, tk):
    cq = MLA_CHAIN
    n_chains = seq // cq
    ri = lax.broadcasted_iota(jnp.int32, (cq, cq), 0)
    ci = lax.broadcasted_iota(jnp.int32, (cq, cq), 1)
    diag_bias = jnp.where(ci <= ri, 0.0, NEG_INF).astype(F32)

    plans = []
    for c in range(n_chains):
        n_full = (c * cq) // tk
        plan = [(j * tk, tk, None) for j in range(n_full)]
        plan.append((n_full * tk, (c + 1) * cq - n_full * tk, diag_bias))
        plans.append(plan)

    state = [None] * n_chains
    for t in range(max(len(p) for p in plans)):
        for c in reversed(range(n_chains)):
            if t >= len(plans[c]):
                continue
            start, nk, bias = plans[c][t]
            rows = slice(c * cq, (c + 1) * cq)
            ks = slice(start, start + nk)
            state[c] = _softmax_block(q_ref[rows, :], k_ref[ks, :], v_ref[ks, :], bias, state[c])
            if t == len(plans[c]) - 1:
                _, l, acc = state[c]
                o_ref[rows, :] = (acc / l).astype(o_ref.dtype)


def _mla_attn(q, k, v, cast_weights, *, tk):
    B, n_heads, S, _ = q.shape
    assert tk % MLA_CHAIN == 0 and S % tk == 0
    head = lambda width: pl.BlockSpec((None, None, S, width), lambda b, h: (b, h, 0, 0))
    cast_specs, cast_shapes = _cast_specs(cast_weights, B * n_heads,
                                          lambda b, h: (b * n_heads + h, 0))
    return pl.pallas_call(
        functools.partial(_mla_attn_kernel, seq=S, tk=tk),
        grid=(B, n_heads),
        in_specs=[head(MLA_QK_PAD), head(MLA_QK_PAD), head(V_HEAD_DIM)] + cast_specs,
        out_specs=[pl.BlockSpec((None, S, V_HEAD_DIM), lambda b, h: (b, 0, h))] + cast_specs,
        out_shape=[jax.ShapeDtypeStruct((B, S, n_heads * V_HEAD_DIM), BF16)] + cast_shapes,
        compiler_params=pltpu.CompilerParams(
            dimension_semantics=("parallel", "parallel"), vmem_limit_bytes=VMEM_LIMIT),
        name="mla_attn",
    )(q, k, v, *cast_weights)


def _outproj_kernel(a_ref, b_ref, x_ref, w_ref, g_ref, o_ref, *, a_width):
    tm = x_ref.shape[0]
    for rows in _row_groups(tm):
        y = jnp.dot(a_ref[rows, :], w_ref[0:a_width, :], preferred_element_type=F32)
        y = y + jnp.dot(b_ref[rows, :], w_ref[a_width:, :], preferred_element_type=F32)
        o_ref[rows, :] = x_ref[rows, :] + _rms(y, g_ref[...])


def _outproj(a, b, x2d, w_out, g_post, *, tm):
    T, D = x2d.shape
    const = lambda i: (0, 0)
    row = lambda i: (i, 0)
    return pl.pallas_call(
        functools.partial(_outproj_kernel, a_width=a.shape[1]),
        grid=(T // tm,),
        in_specs=[
            pl.BlockSpec((tm, a.shape[1]), row),
            pl.BlockSpec((tm, b.shape[1]), row),
            pl.BlockSpec((tm, D), row),
            pl.BlockSpec(w_out.shape, const, pipeline_mode=pl.Buffered(1)),
            pl.BlockSpec((1, D), const),
        ],
        out_specs=pl.BlockSpec((tm, D), row),
        out_shape=jax.ShapeDtypeStruct((T, D), F32),
        compiler_params=pltpu.CompilerParams(
            dimension_semantics=("parallel",), vmem_limit_bytes=VMEM_LIMIT),
        name="outproj",
    )(a, b, x2d, w_out, g_post)


def _mlp_kernel(x_ref, gpre_ref, gpost_ref, wu_ref, wd_ref, o_ref, h_sc):
    f = pl.program_id(1)
    last = pl.num_programs(1) - 1
    tm = x_ref.shape[0]

    def step(first, final):
        for rows in _row_groups(tm):
            if first:
                h = _rms(x_ref[rows, :], gpre_ref[...]).astype(BF16)
                h_sc[rows, :] = h
            else:
                h = h_sc[rows, :]
            u = jnp.dot(h, wu_ref[...], preferred_element_type=F32)
            u = jnp.square(jnp.maximum(u, 0.0)).astype(BF16)
            y = jnp.dot(u, wd_ref[...], preferred_element_type=F32)
            acc = y if first else o_ref[rows, :] + y
            if final:
                o_ref[rows, :] = x_ref[rows, :] + _rms(acc, gpost_ref[...])
            else:
                o_ref[rows, :] = acc

    pl.when(f == 0)(lambda: step(True, False))
    pl.when(jnp.logical_and(f > 0, f < last))(lambda: step(False, False))
    pl.when(f == last)(lambda: step(False, True))


def _mlp(x2d, g_pre, g_post, w_up, w_down, *, tm, tf):
    T, D = x2d.shape
    F = w_up.shape[1]
    return pl.pallas_call(
        _mlp_kernel,
        grid=(T // tm, F // tf),
        in_specs=[
            pl.BlockSpec((tm, D), lambda i, f: (i, 0)),
            pl.BlockSpec((1, D), lambda i, f: (0, 0)),
            pl.BlockSpec((1, D), lambda i, f: (0, 0)),
            pl.BlockSpec((D, tf), lambda i, f: (0, f)),
            pl.BlockSpec((tf, D), lambda i, f: (f, 0)),
        ],
        out_specs=pl.BlockSpec((tm, D), lambda i, f: (i, 0)),
        out_shape=jax.ShapeDtypeStruct((T, D), F32),
        scratch_shapes=[pltpu.VMEM((tm, D), BF16)],
        compiler_params=pltpu.CompilerParams(
            dimension_semantics=("parallel", "arbitrary"), vmem_limit_bytes=VMEM_LIMIT),
        name="mlp",
    )(x2d, g_pre, g_post, w_up, w_down)


def _inv_freq_lanes(rot_dim):
    inv = ROPE_THETA ** (-jnp.arange(0, rot_dim, 2, dtype=F32) / rot_dim)
    lanes = jnp.concatenate([inv, inv, jnp.zeros((LANES - rot_dim,), F32)])
    return lanes.reshape(1, LANES)


def _layer(x2d, pos, B, S, norm_attn_pre, norm_attn_post, w_in, q_latent_norm, kv_latent_norm,
           w_uq, w_ukv, w_out, norm_mlp_pre, norm_mlp_post, w_up, w_down):
    D = x2d.shape[1]
    q_rank = q_latent_norm.shape[0]
    kv_rank = kv_latent_norm.shape[0]
    n_mla = w_uq.shape[1] // (QK_NOPE_DIM + QK_ROPE_DIM)
    a_width = (w_in.shape[1] - q_rank - kv_rank - QK_ROPE_DIM) // 3

    w_in_p = w_in.astype(BF16)
    wq_p = w_uq.astype(BF16)
    wkv_p = w_ukv.astype(BF16)

    invfa = _inv_freq_lanes(ROT_DIM)
    invfb = _inv_freq_lanes(QK_ROPE_DIM)
    row = lambda g: g.reshape(1, -1)

    aq, ak, av, cq, ckv, kr, cos_b, sin_b = _inproj(
        x2d, pos, row(norm_attn_pre), w_in_p, invfa, invfb, row(q_latent_norm),
        row(kv_latent_norm), batch=B, a_width=a_width, q_rank=q_rank, kv_rank=kv_rank, tm=512)
    q_b, k_b, v_b = _mla_up(cq, ckv, kr, cos_b, sin_b, wq_p, wkv_p, batch=B, n_heads=n_mla,
                            tm=1024)

    a_out = _dswa(aq, ak, av)
    b_out, w_out_b, w_up_b, w_down_b = _mla_attn(q_b, k_b, v_b, [w_out, w_up, w_down], tk=512)

    x2d = _outproj(a_out.reshape(B * S, -1), b_out.reshape(B * S, -1), x2d,
                   w_out_b, row(norm_attn_post), tm=1024)
    return _mlp(x2d, row(norm_mlp_pre), row(norm_mlp_post), w_up_b, w_down_b, tm=512, tf=2048)


def kernel(x, positions, norm_attn_pre, norm_attn_post, w_in, q_latent_norm, kv_latent_norm,
           w_uq, w_ukv, w_out, norm_mlp_pre, norm_mlp_post, w_up, w_down):
    B, S, D = x.shape
    x2d = x.reshape(B * S, D)
    pos = positions.astype(F32).reshape(B * S, 1)
    for layer in range(w_in.shape[0]):
        x2d = _layer(x2d, pos, B, S, norm_attn_pre[layer], norm_attn_post[layer], w_in[layer],
                     q_latent_norm[layer], kv_latent_norm[layer], w_uq[layer], w_ukv[layer],
                     w_out[layer], norm_mlp_pre[layer], norm_mlp_post[layer], w_up[layer],
                     w_down[layer])
    return x2d.reshape(B, S, D)
```

```python
import functools

import jax
import jax.numpy as jnp
from jax import lax
from jax.experimental import pallas as pl
from jax.experimental.pallas import tpu as pltpu

F32 = jnp.float32
BF16 = jnp.bfloat16

LANES = 128
HEAD_DIM = 128
ROT_DIM = HEAD_DIM // 4
ROPE_THETA = 500000.0
QK_NOPE_DIM = 128
QK_ROPE_DIM = 64
V_HEAD_DIM = 128
MLA_QK_PAD = 256
Q_BLOCK = 128
DILATIONS = (1, 4, 16)
NORM_EPS = 1e-6
NEG_INF = -1e30
LOG2E = 1.4426950408889634
VMEM_LIMIT = 56 * 1024 * 1024


def _rms(x, gain):
    ms = jnp.mean(x * x, axis=-1, keepdims=True)
    return x * lax.rsqrt(ms + NORM_EPS) * gain


def _rope_tables(pos, inv_freq, half):
    lane = lax.broadcasted_iota(jnp.int32, (1, LANES), 1)
    ang = pos * inv_freq
    sign = jnp.where(lane < half, -1.0, 1.0).astype(F32)
    return jnp.cos(ang), jnp.sin(ang) * sign


def _rope(r, cos, sin, half):
    lane = lax.broadcasted_iota(jnp.int32, (1, LANES), 1)
    partner = jnp.where(lane < half,
                        pltpu.roll(r, LANES - half, 1),
                        pltpu.roll(r, half, 1))
    return r * cos + partner * sin


ROW_GROUP = 256


def _row_groups(tm):
    assert tm % ROW_GROUP == 0
    return [slice(r, r + ROW_GROUP) for r in range(0, tm, ROW_GROUP)]


def _inproj_kernel(x_ref, pos_ref, g_ref, w_ref, invfa_ref, invfb_ref, gq_ref, gkv_ref,
                   aq_ref, ak_ref, av_ref, cq_ref, ckv_ref, kr_ref, cosb_ref, sinb_ref, *,
                   a_width, q_rank, kv_rank):
    n_heads = a_width // HEAD_DIM
    q_scale = HEAD_DIM ** -0.5 * LOG2E
    tm = x_ref.shape[0]
    for rows in _row_groups(tm):
        h = _rms(x_ref[rows, :], g_ref[...]).astype(BF16)
        pos = pos_ref[rows, :]
        cos_a, sin_a = _rope_tables(pos, invfa_ref[...], ROT_DIM // 2)
        cos_b, sin_b = _rope_tables(pos, invfb_ref[...], QK_ROPE_DIM // 2)
        cosb_ref[rows, :] = cos_b
        sinb_ref[rows, :] = sin_b

        r = jnp.dot(h, w_ref[:, 0:a_width], preferred_element_type=F32)
        for hd in range(n_heads):
            sl = slice(hd * HEAD_DIM, (hd + 1) * HEAD_DIM)
            aq_ref[hd, rows, :] = _rope(r[:, sl], cos_a, sin_a, ROT_DIM // 2) * q_scale
        r = jnp.dot(h, w_ref[:, a_width:2 * a_width], preferred_element_type=F32)
        for hd in range(n_heads):
            sl = slice(hd * HEAD_DIM, (hd + 1) * HEAD_DIM)
            ak_ref[hd, rows, :] = _rope(r[:, sl], cos_a, sin_a, ROT_DIM // 2)
        r = jnp.dot(h, w_ref[:, 2 * a_width:3 * a_width], preferred_element_type=F32)
        for hd in range(n_heads):
            av_ref[hd, rows, :] = r[:, hd * HEAD_DIM:(hd + 1) * HEAD_DIM]

        c0 = 3 * a_width
        c1 = c0 + q_rank + kv_rank
        r = jnp.dot(h, w_ref[:, c0:c1], preferred_element_type=F32)
        cq_ref[rows, :] = _rms(r[:, 0:q_rank], gq_ref[...]).astype(BF16)
        ckv_ref[rows, :] = _rms(r[:, q_rank:q_rank + kv_rank], gkv_ref[...]).astype(BF16)
        kr = jnp.dot(h, w_ref[:, c1:], preferred_element_type=F32)
        kr = jnp.concatenate([kr, jnp.zeros((kr.shape[0], LANES - kr.shape[1]), F32)], axis=1)
        kr_ref[rows, :] = _rope(kr, cos_b, sin_b, QK_ROPE_DIM // 2).astype(BF16)


def _inproj(x2d, pos, g_pre, w_in, invfa, invfb, gq, gkv, *, batch, a_width, q_rank, kv_rank, tm):
    T, D = x2d.shape
    S = T // batch
    assert S % tm == 0
    tiles = S // tm
    n_heads = a_width // HEAD_DIM
    ncols = w_in.shape[1]
    const = lambda i: (0, 0)
    row = lambda i: (i, 0)
    head_major = pl.BlockSpec((None, n_heads, tm, HEAD_DIM),
                              lambda i: (i // tiles, 0, i % tiles, 0))
    head_shape = jax.ShapeDtypeStruct((batch, n_heads, S, HEAD_DIM), F32)
    kern = functools.partial(_inproj_kernel, a_width=a_width, q_rank=q_rank, kv_rank=kv_rank)
    return pl.pallas_call(
        kern,
        grid=(T // tm,),
        in_specs=[
            pl.BlockSpec((tm, D), row),
            pl.BlockSpec((tm, 1), row),
            pl.BlockSpec((1, D), const),
            pl.BlockSpec((D, ncols), const, pipeline_mode=pl.Buffered(1)),
            pl.BlockSpec((1, LANES), const),
            pl.BlockSpec((1, LANES), const),
            pl.BlockSpec((1, q_rank), const),
            pl.BlockSpec((1, kv_rank), const),
        ],
        out_specs=[
            head_major, head_major, head_major,
            pl.BlockSpec((tm, q_rank), row),
            pl.BlockSpec((tm, kv_rank), row),
            pl.BlockSpec((tm, LANES), row),
            pl.BlockSpec((tm, LANES), row),
            pl.BlockSpec((tm, LANES), row),
        ],
        out_shape=[
            head_shape, head_shape, head_shape,
            jax.ShapeDtypeStruct((T, q_rank), BF16),
            jax.ShapeDtypeStruct((T, kv_rank), BF16),
            jax.ShapeDtypeStruct((T, LANES), BF16),
            jax.ShapeDtypeStruct((T, LANES), F32),
            jax.ShapeDtypeStruct((T, LANES), F32),
        ],
        compiler_params=pltpu.CompilerParams(
            dimension_semantics=("parallel",), vmem_limit_bytes=VMEM_LIMIT),
        name="inproj",
    )(x2d, pos, g_pre, w_in, invfa, invfb, gq, gkv)


def _mla_up_kernel(cq_ref, ckv_ref, kr_ref, cosb_ref, sinb_ref, wq_ref, wkv_ref,
                   q_ref, k_ref, v_ref, *, n_heads):
    scale = (QK_NOPE_DIM + QK_ROPE_DIM) ** -0.5 * LOG2E
    tm = cq_ref.shape[0]
    for rows in _row_groups(tm):
        cos_b, sin_b = cosb_ref[rows, :], sinb_ref[rows, :]
        rq = jnp.dot(cq_ref[rows, :], wq_ref[...], preferred_element_type=F32)
        rkv = jnp.dot(ckv_ref[rows, :], wkv_ref[...], preferred_element_type=F32)
        kr = kr_ref[rows, :]
        rope_pad = jnp.zeros((rq.shape[0], LANES - QK_ROPE_DIM), F32)
        for hd in range(n_heads):
            base = hd * (QK_NOPE_DIM + QK_ROPE_DIM)
            q_ref[hd, rows, 0:LANES] = (rq[:, base:base + QK_NOPE_DIM] * scale).astype(BF16)
            q_rope = jnp.concatenate(
                [rq[:, base + QK_NOPE_DIM:base + QK_NOPE_DIM + QK_ROPE_DIM], rope_pad], axis=1)
            q_rot = _rope(q_rope, cos_b, sin_b, QK_ROPE_DIM // 2)
            q_ref[hd, rows, LANES:2 * LANES] = (q_rot * scale).astype(BF16)
            kv0 = hd * (QK_NOPE_DIM + V_HEAD_DIM)
            k_ref[hd, rows, 0:LANES] = rkv[:, kv0:kv0 + QK_NOPE_DIM].astype(BF16)
            k_ref[hd, rows, LANES:2 * LANES] = kr
            v_ref[hd, rows, :] = rkv[:, kv0 + QK_NOPE_DIM:kv0 + QK_NOPE_DIM + V_HEAD_DIM].astype(BF16)


def _mla_up(cq, ckv, kr, cos_b, sin_b, wq, wkv, *, batch, n_heads, tm):
    T = cq.shape[0]
    S = T // batch
    assert S % tm == 0
    tiles = S // tm
    const = lambda i: (0, 0)
    row = lambda i: (i, 0)
    head_major = lambda width: pl.BlockSpec((None, n_heads, tm, width),
                                            lambda i: (i // tiles, 0, i % tiles, 0))
    head_shape = lambda width: jax.ShapeDtypeStruct((batch, n_heads, S, width), BF16)
    return pl.pallas_call(
        functools.partial(_mla_up_kernel, n_heads=n_heads),
        grid=(T // tm,),
        in_specs=[
            pl.BlockSpec((tm, cq.shape[1]), row),
            pl.BlockSpec((tm, ckv.shape[1]), row),
            pl.BlockSpec((tm, LANES), row),
            pl.BlockSpec((tm, LANES), row),
            pl.BlockSpec((tm, LANES), row),
            pl.BlockSpec(wq.shape, const),
            pl.BlockSpec(wkv.shape, const),
        ],
        out_specs=[head_major(MLA_QK_PAD), head_major(MLA_QK_PAD), head_major(V_HEAD_DIM)],
        out_shape=[head_shape(MLA_QK_PAD), head_shape(MLA_QK_PAD), head_shape(V_HEAD_DIM)],
        compiler_params=pltpu.CompilerParams(
            dimension_semantics=("parallel",), vmem_limit_bytes=VMEM_LIMIT),
        name="mla_up",
    )(cq, ckv, kr, cos_b, sin_b, wq, wkv)


MAX_DIL = 16
NT_DIMS = (((1,), (1,)), ((), ()))


def _softmax_block(q, k, v, bias, prev):
    nk = k.shape[0]
    s = lax.dot_general(q, k, NT_DIMS, preferred_element_type=F32)
    blocks = [s[:, i * LANES:(i + 1) * LANES] for i in range(nk // LANES)]
    if bias is not None:
        nb = bias.shape[1] // LANES
        blocks[-nb:] = [b + bias[:, i * LANES:(i + 1) * LANES]
                        for i, b in enumerate(blocks[-nb:])]
    m_cur = jnp.max(functools.reduce(jnp.maximum, blocks), axis=1, keepdims=True)
    if prev is None:
        m_new = jnp.broadcast_to(m_cur, (q.shape[0], LANES))
    else:
        m_prev, l_prev, acc_prev = prev
        m_new = jnp.maximum(m_prev, m_cur)
        alpha = jnp.exp2(m_prev - m_new)
    p = jnp.concatenate([jnp.exp2(b - m_new) for b in blocks], axis=1).astype(BF16)
    v_aug = jnp.concatenate([v, jnp.ones((nk, LANES), BF16)], axis=1)
    pv = jnp.dot(p, v_aug, preferred_element_type=F32)
    acc_cur, l_cur = pv[:, :LANES], pv[:, LANES:]
    if prev is None:
        return m_new, l_cur, acc_cur
    return m_new, alpha * l_prev + l_cur, alpha * acc_prev + acc_cur


def _dswa_kernel(q_ref, k_ref, v_ref, o_ref, xq, xk, xv, mx, lx, ax, mn, ln, an, x4, *, seq):
    U = seq // MAX_DIL
    QB = Q_BLOCK
    NEG = NEG_INF

    def band_bias(dist):
        return jnp.where(dist >= 0, jnp.where(dist <= QB, 0.0, NEG), NEG).astype(F32)

    i1 = lax.broadcasted_iota(jnp.int32, (QB, QB), 0)
    j1 = lax.broadcasted_iota(jnp.int32, (QB, QB), 1)
    i2 = lax.broadcasted_iota(jnp.int32, (QB, 2 * QB), 0)
    j2 = lax.broadcasted_iota(jnp.int32, (QB, 2 * QB), 1)
    bias_first = band_bias(i1 - j1)
    bias_band = band_bias(i2 + QB - j2)
    qc = QB // 4
    lo = lambda x, n: x & (n - 1)
    hi = lambda x, n: x >> (n.bit_length() - 1)
    d4_first = band_bias(4 * (lo(i1, qc) - lo(j1, qc)) + hi(i1, qc) - hi(j1, qc))
    d4_band = band_bias(4 * (lo(i2, qc) - lo(j2, 2 * qc) + qc) + hi(i2, qc) - hi(j2, 2 * qc))

    for src, dst in ((q_ref, xq), (k_ref, xk), (v_ref, xv)):
        for r4 in range(4):
            x4[r4] = src[pl.ds(r4, 4 * U, stride=4), :]
        for r in range(MAX_DIL):
            dst[r] = x4[r % 4, pl.ds(r // 4, U, stride=4), :].astype(BF16)

    for r in range(MAX_DIL):
        for n in range(U // QB):
            qs = slice(n * QB, (n + 1) * QB)
            ks = qs if n == 0 else slice((n - 1) * QB, (n + 1) * QB)
            res = _softmax_block(xq[r, qs, :], xk[r, ks, :], xv[r, ks, :],
                                 bias_first if n == 0 else bias_band, None)
            for ref, val in zip((mx, lx, ax), res):
                ref[r, qs, :] = val

    def d4_tile(r4, a, first):
        def gather(ref, start, size):
            return jnp.concatenate([ref[4 * qq + r4, start:start + size, :] for qq in range(4)],
                                   axis=0)
        q = gather(xq, a, qc)
        prev = (gather(mx, a, qc), gather(lx, a, qc), gather(ax, a, qc))
        if first:
            k, v, bias = gather(xk, a, qc), gather(xv, a, qc), d4_first
        else:
            k, v, bias = gather(xk, a - qc, 2 * qc), gather(xv, a - qc, 2 * qc), d4_band
        return _softmax_block(q, k, v, bias, prev)

    for a in range(0, U, qc):
        for r4 in range(4):
            res = d4_tile(r4, a, a == 0)
            for ref, val in zip((mx, lx, ax), res):
                for qq in range(4):
                    ref[4 * qq + r4, a:a + qc, :] = val[qq * qc:(qq + 1) * qc]

    for src, dst in ((mx, mn), (lx, ln), (ax, an)):
        for r in range(MAX_DIL):
            x4[r % 4, pl.ds(r // 4, U, stride=4), :] = src[r]
        for r4 in range(4):
            dst[pl.ds(r4, 4 * U, stride=4), :] = x4[r4]

    for n in range(seq // QB):
        qs = slice(n * QB, (n + 1) * QB)
        ks = qs if n == 0 else slice((n - 1) * QB, (n + 1) * QB)
        prev = (mn[qs, :], ln[qs, :], an[qs, :])
        _, l_new, acc_new = _softmax_block(
            q_ref[qs, :].astype(BF16), k_ref[ks, :].astype(BF16), v_ref[ks, :].astype(BF16),
            bias_first if n == 0 else bias_band, prev)
        o_ref[qs, :] = (acc_new / l_new).astype(o_ref.dtype)


DSWA_HEADS = 2


def _dswa_heads_kernel(q_ref, k_ref, v_ref, o_ref, *scratch, seq):
    for hh in range(DSWA_HEADS):
        _dswa_kernel(q_ref.at[hh], k_ref.at[hh], v_ref.at[hh],
                     o_ref.at[:, hh * HEAD_DIM:(hh + 1) * HEAD_DIM], *scratch, seq=seq)


def _dswa(aq, ak, av):
    B, n_heads, S, _ = aq.shape
    U = S // MAX_DIL
    assert DILATIONS == (1, 4, MAX_DIL) and U % Q_BLOCK == 0 and n_heads % DSWA_HEADS == 0
    spec = pl.BlockSpec((None, DSWA_HEADS, S, HEAD_DIM), lambda b, h: (b, h, 0, 0))
    x16 = lambda dt: pltpu.VMEM((MAX_DIL, U, LANES), dt)
    return pl.pallas_call(
        functools.partial(_dswa_heads_kernel, seq=S),
        grid=(B, n_heads // DSWA_HEADS),
        in_specs=[spec, spec, spec],
        out_specs=pl.BlockSpec((None, S, DSWA_HEADS * HEAD_DIM), lambda b, h: (b, 0, h)),
        out_shape=jax.ShapeDtypeStruct((B, S, n_heads * HEAD_DIM), BF16),
        scratch_shapes=([x16(BF16)] * 3 + [x16(F32)] * 3 + [pltpu.VMEM((S, LANES), F32)] * 3
                        + [pltpu.VMEM((4, S // 4, LANES), F32)]),
        compiler_params=pltpu.CompilerParams(
            dimension_semantics=("parallel", "parallel"), vmem_limit_bytes=VMEM_LIMIT),
        name="dswa",
    )(aq, ak, av)


MLA_CHAIN = 256


def _cast_specs(weights, n_steps, index):
    specs, shapes = [], []
    for w in weights:
        rows = w.shape[0] // n_steps
        assert rows * n_steps == w.shape[0] and rows % 16 == 0
        specs.append(pl.BlockSpec((rows, w.shape[1]), index))
        shapes.append(jax.ShapeDtypeStruct(w.shape, BF16))
    return specs, shapes


def _mla_attn_kernel(q_ref, k_ref, v_ref, *rest, seq, tk):
    n_cast = (len(rest) - 1) // 2
    o_ref = rest[n_cast]
    for src, dst in zip(rest[:n_cast], rest[n_cast + 1:]):
        dst[...] = src[...].astype(BF16)
    _mla_attn_body(q_ref, k_ref, v_ref, o_ref, seq=seq, tk=tk)


def _mla_attn_body(q_ref, k_ref, v_ref, o_ref, *, seq, tk):
    cq = MLA_CHAIN
    n_chains = seq // cq
    ri = lax.broadcasted_iota(jnp.int32, (cq, cq), 0)
    ci = lax.broadcasted_iota(jnp.int32, (cq, cq), 1)
    diag_bias = jnp.where(ci <= ri, 0.0, NEG_INF).astype(F32)

    plans = []
    for c in range(n_chains):
        n_full = (c * cq) // tk
        plan = [(j * tk, tk, None) for j in range(n_full)]
        plan.append((n_full * tk, (c + 1) * cq - n_full * tk, diag_bias))
        plans.append(plan)

    state = [None] * n_chains
    for t in range(max(len(p) for p in plans)):
        for c in reversed(range(n_chains)):
            if t >= len(plans[c]):
                continue
            start, nk, bias = plans[c][t]
            rows = slice(c * cq, (c + 1) * cq)
            ks = slice(start, start + nk)
            state[c] = _softmax_block(q_ref[rows, :], k_ref[ks, :], v_ref[ks, :], bias, state[c])
            if t == len(plans[c]) - 1:
                _, l, acc = state[c]
                o_ref[rows, :] = (acc / l).astype(o_ref.dtype)


def _mla_attn(q, k, v, cast_weights, *, tk):
    B, n_heads, S, _ = q.shape
    assert tk % MLA_CHAIN == 0 and S % tk == 0
    head = lambda width: pl.BlockSpec((None, None, S, width), lambda b, h: (b, h, 0, 0))
    cast_specs, cast_shapes = _cast_specs(cast_weights, B * n_heads,
                                          lambda b, h: (b * n_heads + h, 0))
    return pl.pallas_call(
        functools.partial(_mla_attn_kernel, seq=S, tk=tk),
        grid=(B, n_heads),
        in_specs=[head(MLA_QK_PAD), head(MLA_QK_PAD), head(V_HEAD_DIM)] + cast_specs,
        out_specs=[pl.BlockSpec((None, S, V_HEAD_DIM), lambda b, h: (b, 0, h))] + cast_specs,
        out_shape=[jax.ShapeDtypeStruct((B, S, n_heads * V_HEAD_DIM), BF16)] + cast_shapes,
        compiler_params=pltpu.CompilerParams(
            dimension_semantics=("parallel", "parallel"), vmem_limit_bytes=VMEM_LIMIT),
        name="mla_attn",
    )(q, k, v, *cast_weights)


def _outproj_kernel(a_ref, b_ref, x_ref, w_ref, g_ref, o_ref, *, a_width):
    tm = x_ref.shape[0]
    for rows in _row_groups(tm):
        y = jnp.dot(a_ref[rows, :], w_ref[0:a_width, :], preferred_element_type=F32)
        y = y + jnp.dot(b_ref[rows, :], w_ref[a_width:, :], preferred_element_type=F32)
        o_ref[rows, :] = x_ref[rows, :] + _rms(y, g_ref[...])


def _outproj(a, b, x2d, w_out, g_post, *, tm):
    T, D = x2d.shape
    const = lambda i: (0, 0)
    row = lambda i: (i, 0)
    stream = lambda width: pl.BlockSpec((tm, width), row, pipeline_mode=pl.Buffered(3))
    pipeline = pltpu.emit_pipeline(
        functools.partial(_outproj_kernel, a_width=a.shape[1]),
        grid=(T // tm,),
        in_specs=[stream(a.shape[1]), stream(b.shape[1]), stream(D),
                  pl.BlockSpec(w_out.shape, const), pl.BlockSpec((1, D), const)],
        out_specs=[pl.BlockSpec((tm, D), row)],
    )
    any_space = pl.BlockSpec(memory_space=pl.ANY)
    return pl.pallas_call(
        lambda *refs: pipeline(*refs),
        in_specs=[any_space] * 5,
        out_specs=any_space,
        out_shape=jax.ShapeDtypeStruct((T, D), F32),
        compiler_params=pltpu.CompilerParams(vmem_limit_bytes=VMEM_LIMIT),
        name="outproj",
    )(a, b, x2d, w_out, g_post)


def _mlp_kernel(x_ref, gpre_ref, gpost_ref, wu_ref, wd_ref, o_ref, h_sc):
    f = pl.program_id(1)
    last = pl.num_programs(1) - 1
    tm = x_ref.shape[0]

    def step(first, final):
        for rows in _row_groups(tm):
            if first:
                h = _rms(x_ref[rows, :], gpre_ref[...]).astype(BF16)
                h_sc[rows, :] = h
            else:
                h = h_sc[rows, :]
            u = jnp.dot(h, wu_ref[...], preferred_element_type=F32)
            u = jnp.square(jnp.maximum(u, 0.0)).astype(BF16)
            y = jnp.dot(u, wd_ref[...], preferred_element_type=F32)
            acc = y if first else o_ref[rows, :] + y
            if final:
                o_ref[rows, :] = x_ref[rows, :] + _rms(acc, gpost_ref[...])
            else:
                o_ref[rows, :] = acc

    pl.when(f == 0)(lambda: step(True, False))
    pl.when(jnp.logical_and(f > 0, f < last))(lambda: step(False, False))
    pl.when(f == last)(lambda: step(False, True))


def _mlp(x2d, g_pre, g_post, w_up, w_down, *, tm, tf):
    T, D = x2d.shape
    F = w_up.shape[1]
    return pl.pallas_call(
        _mlp_kernel,
        grid=(T // tm, F // tf),
        in_specs=[
            pl.BlockSpec((tm, D), lambda i, f: (i, 0)),
            pl.BlockSpec((1, D), lambda i, f: (0, 0)),
            pl.BlockSpec((1, D), lambda i, f: (0, 0)),
            pl.BlockSpec((D, tf), lambda i, f: (0, f)),
            pl.BlockSpec((tf, D), lambda i, f: (f, 0)),
        ],
        out_specs=pl.BlockSpec((tm, D), lambda i, f: (i, 0)),
        out_shape=jax.ShapeDtypeStruct((T, D), F32),
        scratch_shapes=[pltpu.VMEM((tm, D), BF16)],
        compiler_params=pltpu.CompilerParams(
            dimension_semantics=("parallel", "arbitrary"), vmem_limit_bytes=VMEM_LIMIT),
        name="mlp",
    )(x2d, g_pre, g_post, w_up, w_down)


def _inv_freq_lanes(rot_dim):
    inv = ROPE_THETA ** (-jnp.arange(0, rot_dim, 2, dtype=F32) / rot_dim)
    lanes = jnp.concatenate([inv, inv, jnp.zeros((LANES - rot_dim,), F32)])
    return lanes.reshape(1, LANES)


def _layer(x2d, pos, B, S, norm_attn_pre, norm_attn_post, w_in, q_latent_norm, kv_latent_norm,
           w_uq, w_ukv, w_out, norm_mlp_pre, norm_mlp_post, w_up, w_down):
    D = x2d.shape[1]
    q_rank = q_latent_norm.shape[0]
    kv_rank = kv_latent_norm.shape[0]
    n_mla = w_uq.shape[1] // (QK_NOPE_DIM + QK_ROPE_DIM)
    a_width = (w_in.shape[1] - q_rank - kv_rank - QK_ROPE_DIM) // 3

    w_in_p = w_in.astype(BF16)
    wq_p = w_uq.astype(BF16)
    wkv_p = w_ukv.astype(BF16)

    invfa = _inv_freq_lanes(ROT_DIM)
    invfb = _inv_freq_lanes(QK_ROPE_DIM)
    row = lambda g: g.reshape(1, -1)

    aq, ak, av, cq, ckv, kr, cos_b, sin_b = _inproj(
        x2d, pos, row(norm_attn_pre), w_in_p, invfa, invfb, row(q_latent_norm),
        row(kv_latent_norm), batch=B, a_width=a_width, q_rank=q_rank, kv_rank=kv_rank, tm=512)
    q_b, k_b, v_b = _mla_up(cq, ckv, kr, cos_b, sin_b, wq_p, wkv_p, batch=B, n_heads=n_mla,
                            tm=1024)

    a_out = _dswa(aq, ak, av)
    b_out, w_out_b, w_up_b, w_down_b = _mla_attn(q_b, k_b, v_b, [w_out, w_up, w_down], tk=512)

    x2d = _outproj(a_out.reshape(B * S, -1), b_out.reshape(B * S, -1), x2d,
                   w_out_b, row(norm_attn_post), tm=512)
    return _mlp(x2d, row(norm_mlp_pre), row(norm_mlp_post), w_up_b, w_down_b, tm=512, tf=2048)


def kernel(x, positions, norm_attn_pre, norm_attn_post, w_in, q_latent_norm, kv_latent_norm,
           w_uq, w_ukv, w_out, norm_mlp_pre, norm_mlp_post, w_up, w_down):
    B, S, D = x.shape
    x2d = x.reshape(B * S, D)
    pos = positions.astype(F32).reshape(B * S, 1)
    for layer in range(w_in.shape[0]):
        x2d = _layer(x2d, pos, B, S, norm_attn_pre[layer], norm_attn_post[layer], w_in[layer],
                     q_latent_norm[layer], kv_latent_norm[layer], w_uq[layer], w_ukv[layer],
                     w_out[layer], norm_mlp_pre[layer], norm_mlp_post[layer], w_up[layer],
                     w_down[layer])
    return x2d.reshape(B, S, D)
```
